```python
import math
import jax, jax.numpy as jnp
from jax import lax
import numpy as np

D_MODEL = 1024
BATCH = 8
SEQ = 4096
DEPTH = 4

N_A_LAYERS = DEPTH // 2
N_B_LAYERS = DEPTH - N_A_LAYERS
A_HEADS = 8
A_HEAD_DIM = D_MODEL // A_HEADS // 2
B_HEADS = 16
B_HEAD_DIM = D_MODEL // B_HEADS
D_FF = 2816
CONV_WIDTH = 3
N_BUCKETS = 32
MAX_DISTANCE = 128
Q_BLOCK = 128
RMS_EPS = 1e-6
SUBLN_EPS = 1e-5
NEG_INF = -1e30

kernel_name = "yoco_diffattn_fox_convffn"


def rms_norm(x, gain, eps=RMS_EPS):
    xf = x.astype(jnp.float32)
    y = xf * lax.rsqrt(jnp.mean(xf * xf, axis=-1, keepdims=True) + eps)
    return (y * gain.astype(jnp.float32)).astype(x.dtype)


def t5_bucket(dist):
    max_exact = N_BUCKETS // 2
    d = jnp.maximum(dist, 0)
    log_ratio = jnp.log(jnp.maximum(d, 1).astype(jnp.float32) / max_exact) / math.log(MAX_DISTANCE / max_exact)
    large = jnp.minimum(max_exact + (log_ratio * (N_BUCKETS - max_exact)).astype(jnp.int32), N_BUCKETS - 1)
    return jnp.where(d < max_exact, d, large)


def lambda_init_fn(layer_idx):
    return 0.8 - 0.6 * math.exp(-0.3 * layer_idx)


def diff_attention(x, w_qkv, w_o, lam_q1, lam_k1, lam_q2, lam_k2, subln_g, rel_bias, lambda_init):
    B, S, _ = x.shape
    qkv = x @ w_qkv
    nq = 2 * A_HEADS * A_HEAD_DIM
    q, k, v = jnp.split(qkv, [nq, 2 * nq], axis=-1)
    q = q.reshape(B, S, A_HEADS, 2, A_HEAD_DIM).astype(jnp.float32) * (A_HEAD_DIM ** -0.5)
    k = k.reshape(B, S, A_HEADS, 2, A_HEAD_DIM).astype(jnp.float32)
    v = v.reshape(B, S, A_HEADS, 2 * A_HEAD_DIM).astype(jnp.float32)
    lam = (jnp.exp(jnp.sum(lam_q1.astype(jnp.float32) * lam_k1.astype(jnp.float32)))
           - jnp.exp(jnp.sum(lam_q2.astype(jnp.float32) * lam_k2.astype(jnp.float32)))
           + lambda_init)
    outs = []
    for i in range(S // Q_BLOCK):
        q0 = i * Q_BLOCK
        k_end = q0 + Q_BLOCK
        dist = jnp.arange(q0, k_end)[:, None] - jnp.arange(k_end)[None, :]
        bias = jnp.moveaxis(rel_bias[t5_bucket(dist)].astype(jnp.float32), -1, 0)
        logits = jnp.einsum('bqhcd,bkhcd->bchqk', q[:, q0:k_end], k[:, :k_end]) + bias[None, None]
        logits = jnp.where(dist >= 0, logits, NEG_INF)
        p = jax.nn.softmax(logits, axis=-1)
        attn = p[:, 0] - lam * p[:, 1]
        outs.append(jnp.einsum('bhqk,bkhe->bqhe', attn, v[:, :k_end]))
    o = jnp.concatenate(outs, axis=1)
    o = rms_norm(o, subln_g, SUBLN_EPS) * (1.0 - lambda_init)
    return o.reshape(B, S, A_HEADS * 2 * A_HEAD_DIM).astype(x.dtype) @ w_o


def shared_kv(h, kv_norm, w_kvf, b_f):
    B, S, _ = h.shape
    kvf = rms_norm(h, kv_norm) @ w_kvf
    k, v, f_logit = jnp.split(kvf, [B_HEADS * B_HEAD_DIM, 2 * B_HEADS * B_HEAD_DIM], axis=-1)
    k = k.reshape(B, S, B_HEADS, B_HEAD_DIM).astype(jnp.float32)
    v = v.reshape(B, S, B_HEADS, B_HEAD_DIM).astype(jnp.float32)
    log_f = jax.nn.log_sigmoid(f_logit.astype(jnp.float32) + b_f.astype(jnp.float32))
    c = jnp.cumsum(log_f, axis=1)
    return k, v, jnp.transpose(c, (0, 2, 1))


def forgetting_attention(x, w_q, w_o, k, v, c):
    B, S, _ = x.shape
    q = (x @ w_q).reshape(B, S, B_HEADS, B_HEAD_DIM).astype(jnp.float32) * (B_HEAD_DIM ** -0.5)
    outs = []
    for i in range(S // Q_BLOCK):
        q0 = i * Q_BLOCK
        k_end = q0 + Q_BLOCK
        dist = jnp.arange(q0, k_end)[:, None] - jnp.arange(k_end)[None, :]
        decay = c[:, :, q0:k_end, None] - c[:, :, None, :k_end]
        logits = jnp.einsum('bqhd,bkhd->bhqk', q[:, q0:k_end], k[:, :k_end]) + decay
        logits = jnp.where(dist >= 0, logits, NEG_INF)
        p = jax.nn.softmax(logits, axis=-1)
        outs.append(jnp.einsum('bhqk,bkhd->bqhd', p, v[:, :k_end]))
    o = jnp.concatenate(outs, axis=1)
    return o.reshape(B, S, B_HEADS * B_HEAD_DIM).astype(x.dtype) @ w_o


def conv_ffn(x, w_in, conv_w, conv_b, w_out):
    u = x @ w_in
    u = lax.conv_general_dilated(u, conv_w[:, None, :].astype(u.dtype), window_strides=(1,),
                                 padding=[(CONV_WIDTH - 1, 0)],
                                 dimension_numbers=('NWC', 'WIO', 'NWC'),
                                 feature_group_count=2 * D_FF) + conv_b
    gate, val = jnp.split(u, 2, axis=-1)
    return (jax.nn.gelu(gate) * val) @ w_out


def setup_inputs(seed: int = 0) -> dict:
    key = jax.random.key(seed)
    ks = iter(jax.random.split(key, 32))
    nrm = lambda shape, s: jax.random.normal(next(ks), shape, jnp.float32) * s
    gain = lambda shape: 1.0 + nrm(shape, 0.05)
    D, F = D_MODEL, D_FF
    return {
        "x": nrm((BATCH, SEQ, D), 1.0),
        "rel_bias": nrm((N_BUCKETS, A_HEADS), 0.2),
        "a_norm_pre": gain((N_A_LAYERS, D)),
        "a_norm_post": gain((N_A_LAYERS, D)),
        "a_w_qkv": nrm((N_A_LAYERS, D, 3 * 2 * A_HEADS * A_HEAD_DIM), D ** -0.5),
        "a_lam_q1": nrm((N_A_LAYERS, A_HEAD_DIM), 0.1),
        "a_lam_k1": nrm((N_A_LAYERS, A_HEAD_DIM), 0.1),
        "a_lam_q2": nrm((N_A_LAYERS, A_HEAD_DIM), 0.1),
        "a_lam_k2": nrm((N_A_LAYERS, A_HEAD_DIM), 0.1),
        "a_subln": gain((N_A_LAYERS, 2 * A_HEAD_DIM)),
        "a_w_o": nrm((N_A_LAYERS, 2 * A_HEADS * A_HEAD_DIM, D), D ** -0.5),
        "kv_norm": gain((D,)),
        "w_kvf": jnp.concatenate([nrm((D, 2 * B_HEADS * B_HEAD_DIM), D ** -0.5),
                                  nrm((D, B_HEADS), 0.1 * D ** -0.5)], axis=-1),
        "b_f": 3.0 + nrm((B_HEADS,), 0.1),
        "b_norm_pre": gain((N_B_LAYERS, D)),
        "b_norm_post": gain((N_B_LAYERS, D)),
        "b_w_q": nrm((N_B_LAYERS, D, B_HEADS * B_HEAD_DIM), D ** -0.5),
        "b_w_o": nrm((N_B_LAYERS, B_HEADS * B_HEAD_DIM, D), D ** -0.5),
        "ffn_norm_pre": gain((DEPTH, D)),
        "ffn_norm_post": gain((DEPTH, D)),
        "ffn_w_in": nrm((DEPTH, D, 2 * F), D ** -0.5),
        "ffn_conv_w": nrm((DEPTH, CONV_WIDTH, 2 * F), CONV_WIDTH ** -0.5),
        "ffn_conv_b": nrm((DEPTH, 2 * F), 0.02),
        "ffn_w_out": nrm((DEPTH, F, D), F ** -0.5),
    }


def reference(x, rel_bias, a_norm_pre, a_norm_post, a_w_qkv, a_lam_q1, a_lam_k1, a_lam_q2, a_lam_k2,
              a_subln, a_w_o, kv_norm, w_kvf, b_f, b_norm_pre, b_norm_post, b_w_q, b_w_o,
              ffn_norm_pre, ffn_norm_post, ffn_w_in, ffn_conv_w, ffn_conv_b, ffn_w_out):
    h = x
    for l in range(DEPTH):
        if l < N_A_LAYERS:
            a = diff_attention(rms_norm(h, a_norm_pre[l]), a_w_qkv[l], a_w_o[l],
                               a_lam_q1[l], a_lam_k1[l], a_lam_q2[l], a_lam_k2[l],
                               a_subln[l], rel_bias, lambda_init_fn(l))
            h = h + rms_norm(a, a_norm_post[l])
        else:
            if l == N_A_LAYERS:
                k_sh, v_sh, c_sh = shared_kv(h, kv_norm, w_kvf, b_f)
            j = l - N_A_LAYERS
            a = forgetting_attention(rms_norm(h, b_norm_pre[j]), b_w_q[j], b_w_o[j], k_sh, v_sh, c_sh)
            h = h + rms_norm(a, b_norm_post[j])
        f = conv_ffn(rms_norm(h, ffn_norm_pre[l]), ffn_w_in[l], ffn_conv_w[l], ffn_conv_b[l], ffn_w_out[l])
        h = h + rms_norm(f, ffn_norm_post[l])
    return h
```

```python
import functools
import math

import jax
import jax.numpy as jnp
from jax import lax
from jax.experimental import pallas as pl
from jax.experimental.pallas import tpu as pltpu

D_MODEL = 1024
SEQ = 4096
DEPTH = 4
N_A_LAYERS = DEPTH // 2
A_HEADS = 8
A_HEAD_DIM = 64
B_HEADS = 16
B_HEAD_DIM = 64
D_FF = 2816
N_BUCKETS = 32
MAX_DISTANCE = 128
RMS_EPS = 1e-6
SUBLN_EPS = 1e-5
NEG_INF = -1e30
LOG2E = math.log2(math.e)

LANES = 128
ATTN_T = 512
ROW_TILE = 512
HALO = 16
FFN_CHUNK = 256
VMEM_LIMIT = 56 * 1024 * 1024

NT_DIMS = (((1,), (1,)), ((), ()))


def _lambda_init(layer_idx):
    return 0.8 - 0.6 * math.exp(-0.3 * layer_idx)


def _rms(x, gain, eps):
    ms = jnp.mean(x * x, axis=-1, keepdims=True)
    return x * lax.rsqrt(ms + eps) * gain


def _resident(shape):
    nd = len(shape)
    return pl.BlockSpec(shape, lambda *_: (0,) * nd, pipeline_mode=pl.Buffered(1))


def _params(sem):
    return pltpu.CompilerParams(dimension_semantics=sem, vmem_limit_bytes=VMEM_LIMIT)


def _proj_kernel(x_ref, g_ref, w_ref, cs_ref, *rest, has_t):
    if has_t:
        wt_ref, o_ref, ot_ref = rest
    else:
        (o_ref,) = rest
    xn = _rms(x_ref[...], g_ref[...], RMS_EPS).astype(jnp.bfloat16)
    y = jnp.dot(xn, w_ref[...], preferred_element_type=jnp.float32)
    o_ref[...] = (y * cs_ref[...]).astype(o_ref.dtype)
    if has_t:
        yt = lax.dot_general(wt_ref[...], xn, NT_DIMS, preferred_element_type=jnp.float32)
        ot_ref[...] = yt.astype(ot_ref.dtype)


def _project(h, gain, w, colscale, wt=None):
    B, S, D = h.shape
    N = w.shape[1]
    tm = ROW_TILE
    in_specs = [
        pl.BlockSpec((None, tm, D), lambda b, i: (b, i, 0)),
        _resident((1, D)),
        _resident((D, N)),
        _resident((1, N)),
    ]
    args = [h, gain.reshape(1, D), w, colscale.reshape(1, N)]
    out_shape = [jax.ShapeDtypeStruct((B, S, N), jnp.bfloat16)]
    out_specs = [pl.BlockSpec((None, tm, N), lambda b, i: (b, i, 0))]
    if wt is not None:
        Nt = wt.shape[0]
        in_specs.append(_resident((Nt, D)))
        args.append(wt)
        out_shape.append(jax.ShapeDtypeStruct((B, Nt, S), jnp.bfloat16))
        out_specs.append(pl.BlockSpec((None, Nt, tm), lambda b, i: (b, 0, i)))
    res = pl.pallas_call(
        functools.partial(_proj_kernel, has_t=wt is not None),
        grid=(B, S // tm),
        in_specs=in_specs,
        out_specs=out_specs,
        out_shape=out_shape,
        compiler_params=_params(("parallel", "parallel")),
        name="norm_proj_t" if wt is not None else "norm_proj",
    )(*args)
    return res if wt is not None else res[0]


def _kvf_kernel(x_ref, g_ref, wk_ref, wvt_ref, wf_ref, bf_ref, k_ref, vt_ref, c_ref, carry_ref):
    @pl.when(pl.program_id(1) == 0)
    def _():
        carry_ref[...] = jnp.zeros_like(carry_ref)

    xn = _rms(x_ref[...], g_ref[...], RMS_EPS).astype(jnp.bfloat16)
    k_ref[...] = jnp.dot(xn, wk_ref[...], preferred_element_type=jnp.float32).astype(k_ref.dtype)
    vt_ref[...] = lax.dot_general(wvt_ref[...], xn, NT_DIMS,
                                  preferred_element_type=jnp.float32).astype(vt_ref.dtype)
    fl = jnp.dot(xn, wf_ref[...], preferred_element_type=jnp.float32) + bf_ref[...]
    ls = jnp.minimum(fl, 0.0) - jnp.log1p(jnp.exp(-jnp.abs(fl)))
    tm = ls.shape[0]
    row = lax.broadcasted_iota(jnp.int32, ls.shape, 0)
    sh = 1
    while sh < tm:
        ls = ls + jnp.where(row >= sh, pltpu.roll(ls, sh, 0), 0.0)
        sh *= 2
    c = ls + carry_ref[...]
    c_ref[...] = c
    carry_ref[...] = c[tm - 1:tm, :]


def _shared_kv(h, gain, w_kvf, b_f):
    B, S, D = h.shape
    HD = B_HEADS * B_HEAD_DIM
    tm = ROW_TILE
    wk = w_kvf[:, :HD].astype(jnp.bfloat16)
    wvt = w_kvf[:, HD:2 * HD].T.astype(jnp.bfloat16)
    wf = jnp.pad(w_kvf[:, 2 * HD:], ((0, 0), (0, LANES - B_HEADS))).astype(jnp.bfloat16)
    bf = jnp.pad(b_f, (0, LANES - B_HEADS)).reshape(1, LANES)
    return pl.pallas_call(
        _kvf_kernel,
        grid=(B, S // tm),
        in_specs=[
            pl.BlockSpec((None, tm, D), lambda b, i: (b, i, 0)),
            _resident((1, D)),
            _resident((D, HD)),
            _resident((HD, D)),
            _resident((D, LANES)),
            _resident((1, LANES)),
        ],
        out_specs=[
            pl.BlockSpec((None, tm, HD), lambda b, i: (b, i, 0)),
            pl.BlockSpec((None, HD, tm), lambda b, i: (b, 0, i)),
            pl.BlockSpec((None, tm, LANES), lambda b, i: (b, i, 0)),
        ],
        out_shape=[
            jax.ShapeDtypeStruct((B, S, HD), jnp.bfloat16),
            jax.ShapeDtypeStruct((B, HD, S), jnp.bfloat16),
            jax.ShapeDtypeStruct((B, S, LANES), jnp.float32),
        ],
        scratch_shapes=[pltpu.VMEM((1, LANES), jnp.float32)],
        compiler_params=_params(("parallel", "arbitrary")),
        name="shared_kvf",
    )(h, gain.reshape(1, D), wk, wvt, wf, bf)


def _bias_kernel(rel_ref, d_ref, bkt_ref, out_ref):
    h = pl.program_id(0)
    d = d_ref[...]
    bkt = bkt_ref[...]
    acc = jnp.zeros(d.shape, jnp.float32)
    for j in range(N_BUCKETS):
        acc = jnp.where(bkt == j, rel_ref[j, h], acc)
    far = rel_ref[N_BUCKETS - 1, h]
    out_ref[...] = jnp.where(d >= 0, (acc - far) * LOG2E, NEG_INF)


def _t5_bucket(dist):
    max_exact = N_BUCKETS // 2
    d = jnp.maximum(dist, 0)
    log_ratio = jnp.log(jnp.maximum(d, 1).astype(jnp.float32) / max_exact) / math.log(MAX_DISTANCE / max_exact)
    large = jnp.minimum(max_exact + (log_ratio * (N_BUCKETS - max_exact)).astype(jnp.int32), N_BUCKETS - 1)
    return jnp.where(d < max_exact, d, large)


def _bias_tables(rel_bias):
    T = ATTN_T
    assert T >= MAX_DISTANCE
    key = jnp.arange(T)[:, None]
    qry = jnp.arange(T)[None, :]
    dist = jnp.stack([qry - key, T + qry - key]).astype(jnp.int32)
    bkt = _t5_bucket(dist).astype(jnp.int32)
    return pl.pallas_call(
        _bias_kernel,
        grid=(A_HEADS,),
        in_specs=[
            pl.BlockSpec(memory_space=pltpu.SMEM),
            _resident((2, T, T)),
            _resident((2, T, T)),
        ],
        out_specs=pl.BlockSpec((None, 2, T, T), lambda h: (h, 0, 0, 0)),
        out_shape=jax.ShapeDtypeStruct((A_HEADS, 2, T, T), jnp.float32),
        compiler_params=_params(("parallel",)),
        name="t5_bias_tiles",
    )(rel_bias, dist, bkt)


def _attn_kernel(*refs, diff, lam_init):
    if diff:
        (q_ref, k_ref, vt_ref, bias_ref, lam_ref, g_ref, o_ref,
         m0_ref, l0_ref, a0_ref, m1_ref, l1_ref, a1_ref) = refs
    else:
        (q_ref, k_ref, vt_ref, c_ref, o_ref,
         m0_ref, l0_ref, a0_ref, m1_ref, l1_ref, a1_ref) = refs
    T = ATTN_T
    half = LANES // 2
    qi = pl.program_id(2)

    q = q_ref[...]
    lane = lax.broadcasted_iota(jnp.int32, q.shape, 1)
    zero = jnp.zeros_like(q)
    qz = (jnp.where(lane < half, q, zero), jnp.where(lane >= half, q, zero))
    streams = ((m0_ref, l0_ref, a0_ref), (m1_ref, l1_ref, a1_ref))
    for m_ref, l_ref, a_ref in streams:
        m_ref[...] = jnp.full_like(m_ref, NEG_INF)
        l_ref[...] = jnp.zeros_like(l_ref)
        a_ref[...] = jnp.zeros_like(a_ref)

    def step(ki, tile_bias):
        off = pl.multiple_of(ki * T, T)
        kt = k_ref[pl.ds(off, T), :]
        vt = vt_ref[:, pl.ds(off, T)]
        if not diff:
            negc = c_ref[pl.ds(off, T), :] * (-LOG2E)
        for j, (m_ref, l_ref, a_ref) in enumerate(streams):
            s = lax.dot_general(kt, qz[j], NT_DIMS, preferred_element_type=jnp.float32)
            if not diff:
                s = s + negc[:, j:j + 1]
            if tile_bias is not None:
                s = s + tile_bias
            m_old = m_ref[...]
            m_new = jnp.maximum(m_old, jnp.max(s, axis=0, keepdims=True))
            p = jnp.exp2(s - m_new)
            alpha = jnp.exp2(m_old - m_new)
            l_ref[...] = alpha * l_ref[...] + jnp.sum(p, axis=0, keepdims=True)
            a_ref[...] = alpha * a_ref[...] + jnp.dot(vt, p.astype(jnp.bfloat16),
                                                      preferred_element_type=jnp.float32)
            m_ref[...] = m_new

    def far_body(ki, carry):
        step(ki, None)
        return carry

    if diff:
        lax.fori_loop(0, jnp.maximum(qi - 1, 0), far_body, 0)

        @pl.when(qi >= 1)
        def _():
            step(qi - 1, bias_ref[1])

        step(qi, bias_ref[0])
    else:
        lax.fori_loop(0, qi, far_body, 0)
        key = lax.broadcasted_iota(jnp.int32, (T, T), 0)
        qry = lax.broadcasted_iota(jnp.int32, (T, T), 1)
        step(qi, jnp.where(key <= qry, 0.0, NEG_INF))

    o0 = a0_ref[...] / l0_ref[...]
    o1 = a1_ref[...] / l1_ref[...]
    if diff:
        lam = (jnp.exp(jnp.sum(lam_ref[0:1, :] * lam_ref[1:2, :], keepdims=True))
               - jnp.exp(jnp.sum(lam_ref[2:3, :] * lam_ref[3:4, :], keepdims=True))
               + lam_init)
        o = o0 - lam * o1
        ms = jnp.mean(o * o, axis=0, keepdims=True)
        o = o * lax.rsqrt(ms + SUBLN_EPS) * g_ref[...] * (1.0 - lam_init)
    else:
        row = lax.broadcasted_iota(jnp.int32, o0.shape, 0)
        o = jnp.where(row < half, o0, o1)
    o_ref[...] = o.T.astype(o_ref.dtype)


def _attn_scratch():
    T = ATTN_T
    one = [pltpu.VMEM((1, T), jnp.float32), pltpu.VMEM((1, T), jnp.float32),
           pltpu.VMEM((LANES, T), jnp.float32)]
    return one + one


def _diff_attention(qk, vt, bias_tab, lam_vecs, subln_g, lam_init):
    B, S, _ = qk.shape
    T = ATTN_T
    H = A_HEADS
    return pl.pallas_call(
        functools.partial(_attn_kernel, diff=True, lam_init=lam_init),
        grid=(H, B, S // T),
        in_specs=[
            pl.BlockSpec((None, T, LANES), lambda h, b, i: (b, i, h)),
            pl.BlockSpec((None, S, LANES), lambda h, b, i: (b, 0, H + h)),
            pl.BlockSpec((None, LANES, S), lambda h, b, i: (b, h, 0)),
            pl.BlockSpec((None, 2, T, T), lambda h, b, i: (h, 0, 0, 0)),
            _resident((4, A_HEAD_DIM)),
            _resident((LANES, 1)),
        ],
        out_specs=pl.BlockSpec((None, T, LANES), lambda h, b, i: (b, i, h)),
        out_shape=jax.ShapeDtypeStruct((B, S, H * LANES), jnp.bfloat16),
        scratch_shapes=_attn_scratch(),
        compiler_params=_params(("parallel", "parallel", "arbitrary")),
        name="diff_attn",
    )(qk, qk, vt, bias_tab, lam_vecs, subln_g.reshape(LANES, 1))


def _fox_attention(q, k, vt, c_pairs):
    B, S, _ = q.shape
    T = ATTN_T
    G = B_HEADS // 2
    return pl.pallas_call(
        functools.partial(_attn_kernel, diff=False, lam_init=0.0),
        grid=(G, B, S // T),
        in_specs=[
            pl.BlockSpec((None, T, LANES), lambda g, b, i: (b, i, g)),
            pl.BlockSpec((None, S, LANES), lambda g, b, i: (b, 0, g)),
            pl.BlockSpec((None, LANES, S), lambda g, b, i: (b, g, 0)),
            pl.BlockSpec((None, None, S, 2), lambda g, b, i: (b, g, 0, 0)),
        ],
        out_specs=pl.BlockSpec((None, T, LANES), lambda g, b, i: (b, i, g)),
        out_shape=jax.ShapeDtypeStruct((B, S, G * LANES), jnp.bfloat16),
        scratch_shapes=_attn_scratch(),
        compiler_params=_params(("parallel", "parallel", "arbitrary")),
        name="fox_attn",
    )(q, k, vt, c_pairs)


def _oproj_kernel(o_ref, w_ref, h_ref, g_ref, out_ref):
    a = jnp.dot(o_ref[...], w_ref[...], preferred_element_type=jnp.float32)
    out_ref[...] = h_ref[...] + _rms(a, g_ref[...], RMS_EPS)


def _out_project(o, w, h, gain):
    B, S, D = h.shape
    tm = ROW_TILE
    row = pl.BlockSpec((None, tm, D), lambda b, i: (b, i, 0))
    return pl.pallas_call(
        _oproj_kernel,
        grid=(B, S // tm),
        in_specs=[row, _resident((D, D)), row, _resident((1, D))],
        out_specs=row,
        out_shape=jax.ShapeDtypeStruct((B, S, D), jnp.float32),
        compiler_params=_params(("parallel", "parallel")),
        name="out_proj",
    )(o, w, h, gain.reshape(1, D))


def _ffn_kernel(h_ref, halo_ref, gpre_ref, win_ref, cw_ref, cb_ref, wout_ref, gpost_ref,
                out_ref, xe_ref, acc_ref):
    F = D_FF
    FC = FFN_CHUNK
    tm = h_ref.shape[0]
    h = h_ref[...]
    gpre = gpre_ref[...]
    first = (pl.program_id(1) == 0).astype(jnp.float32)
    xe_ref[0:HALO, :] = (_rms(halo_ref[...], gpre, RMS_EPS) * (1.0 - first)).astype(xe_ref.dtype)
    xe_ref[HALO:, :] = _rms(h, gpre, RMS_EPS).astype(xe_ref.dtype)
    acc_ref[...] = jnp.zeros_like(acc_ref)

    def conv(u, w, b):
        y = u * w[2:3, :] + pltpu.roll(u, 1, 0) * w[1:2, :] + pltpu.roll(u, 2, 0) * w[0:1, :] + b
        return y[HALO:, :]

    def chunk(c, carry):
        og = pl.multiple_of(c * FC, FC)
        ov = pl.multiple_of(F + c * FC, LANES)
        xe = xe_ref[...]
        ug = jnp.dot(xe, win_ref[:, pl.ds(og, FC)], preferred_element_type=jnp.float32)
        uv = jnp.dot(xe, win_ref[:, pl.ds(ov, FC)], preferred_element_type=jnp.float32)
        gate = conv(ug, cw_ref[:, pl.ds(og, FC)], cb_ref[:, pl.ds(og, FC)])
        val = conv(uv, cw_ref[:, pl.ds(ov, FC)], cb_ref[:, pl.ds(ov, FC)])
        act = (jax.nn.gelu(gate) * val).astype(jnp.bfloat16)
        acc_ref[...] += jnp.dot(act, wout_ref[pl.ds(og, FC), :], preferred_element_type=jnp.float32)
        return carry

    lax.fori_loop(0, F // FC, chunk, 0)
    out_ref[...] = h + _rms(acc_ref[...], gpost_ref[...], RMS_EPS)


def _conv_ffn(h, gpre, w_in, conv_w, conv_b, w_out, gpost):
    B, S, D = h.shape
    F = D_FF
    tm = ROW_TILE
    assert F % FFN_CHUNK == 0 and tm % HALO == 0
    row = pl.BlockSpec((None, tm, D), lambda b, i: (b, i, 0))
    halo = pl.BlockSpec((None, HALO, D), lambda b, i: (b, jnp.maximum(i * (tm // HALO) - 1, 0), 0))
    return pl.pallas_call(
        _ffn_kernel,
        grid=(B, S // tm),
        in_specs=[row, halo, _resident((1, D)), _resident((D, 2 * F)), _resident((3, 2 * F)),
                  _resident((1, 2 * F)), _resident((F, D)), _resident((1, D))],
        out_specs=row,
        out_shape=jax.ShapeDtypeStruct((B, S, D), jnp.float32),
        scratch_shapes=[pltpu.VMEM((tm + HALO, D), jnp.bfloat16), pltpu.VMEM((tm, D), jnp.float32)],
        compiler_params=_params(("parallel", "parallel")),
        name="conv_ffn",
    )(h, h, gpre.reshape(1, D), w_in.astype(jnp.bfloat16), conv_w, conv_b.reshape(1, 2 * F),
      w_out.astype(jnp.bfloat16), gpost.reshape(1, D))


def kernel(x, rel_bias, a_norm_pre, a_norm_post, a_w_qkv, a_lam_q1, a_lam_k1, a_lam_q2, a_lam_k2, a_subln, a_w_o, kv_norm, w_kvf, b_f, b_norm_pre, b_norm_post, b_w_q, b_w_o, ffn_norm_pre, ffn_norm_post, ffn_w_in, ffn_conv_w, ffn_conv_b, ffn_w_out):
    B, S, D = x.shape
    q_scale = LOG2E * A_HEAD_DIM ** -0.5
    nqk = 2 * D
    a_colscale = jnp.concatenate([jnp.full((D,), q_scale, jnp.float32), jnp.ones((D,), jnp.float32)])
    b_colscale = jnp.full((D,), LOG2E * B_HEAD_DIM ** -0.5, jnp.float32)
    bias_tab = _bias_tables(rel_bias)

    h = x
    for l in range(DEPTH):
        if l < N_A_LAYERS:
            w = a_w_qkv[l]
            qk, vt = _project(h, a_norm_pre[l], w[:, :nqk].astype(jnp.bfloat16), a_colscale,
                              wt=w[:, nqk:].T.astype(jnp.bfloat16))
            lam_vecs = jnp.stack([a_lam_q1[l], a_lam_k1[l], a_lam_q2[l], a_lam_k2[l]])
            o = _diff_attention(qk, vt, bias_tab, lam_vecs, a_subln[l], _lambda_init(l))
            h = _out_project(o, a_w_o[l].astype(jnp.bfloat16), h, a_norm_post[l])
        else:
            j = l - N_A_LAYERS
            if j == 0:
                k_sh, vt_sh, c_sh = _shared_kv(h, kv_norm, w_kvf, b_f)
                c_pairs = c_sh[:, :, :B_HEADS].reshape(B, S, B_HEADS // 2, 2).transpose(0, 2, 1, 3)
            q = _project(h, b_norm_pre[j], b_w_q[j].astype(jnp.bfloat16), b_colscale)
            o = _fox_attention(q, k_sh, vt_sh, c_pairs)
            h = _out_project(o, b_w_o[j].astype(jnp.bfloat16), h, b_norm_post[j])
        h = _conv_ffn(h, ffn_norm_pre[l], ffn_w_in[l], ffn_conv_w[l], ffn_conv_b[l], ffn_w_out[l],
                      ffn_norm_post[l])
    return h
```

```python
import functools
import math

import jax
import jax.numpy as jnp
from jax import lax
from jax.experimental import pallas as pl
from jax.experimental.pallas import tpu as pltpu

D_MODEL = 1024
SEQ = 4096
DEPTH = 4
N_A_LAYERS = DEPTH // 2
A_HEADS = 8
A_HEAD_DIM = 64
B_HEADS = 16
B_HEAD_DIM = 64
D_FF = 2816
N_BUCKETS = 32
MAX_DISTANCE = 128
RMS_EPS = 1e-6
SUBLN_EPS = 1e-5
NEG_INF = -1e30
LOG2E = math.log2(math.e)

LANES = 128
ATTN_TQ = 512
ATTN_TK = 256
ROW_TILE = 512
HALO = 16
FFN_CHUNK = 256
VMEM_LIMIT = 56 * 1024 * 1024

NT_DIMS = (((1,), (1,)), ((), ()))


def _lambda_init(layer_idx):
    return 0.8 - 0.6 * math.exp(-0.3 * layer_idx)


def _rms(x, gain, eps):
    ms = jnp.mean(x * x, axis=-1, keepdims=True)
    return x * lax.rsqrt(ms + eps) * gain


def _resident(shape):
    nd = len(shape)
    return pl.BlockSpec(shape, lambda *_: (0,) * nd, pipeline_mode=pl.Buffered(1))


def _params(sem):
    return pltpu.CompilerParams(dimension_semantics=sem, vmem_limit_bytes=VMEM_LIMIT)


def _proj_kernel(x_ref, g_ref, w_ref, cs_ref, *rest, has_t):
    if has_t:
        wt_ref, o_ref, ot_ref = rest
    else:
        (o_ref,) = rest
    xn = _rms(x_ref[...], g_ref[...], RMS_EPS).astype(jnp.bfloat16)
    y = jnp.dot(xn, w_ref[...], preferred_element_type=jnp.float32)
    o_ref[...] = (y * cs_ref[...]).astype(o_ref.dtype)
    if has_t:
        yt = lax.dot_general(wt_ref[...], xn, NT_DIMS, preferred_element_type=jnp.float32)
        ot_ref[...] = yt.astype(ot_ref.dtype)


def _project(h, gain, w, colscale, wt=None):
    B, S, D = h.shape
    N = w.shape[1]
    tm = ROW_TILE
    in_specs = [
        pl.BlockSpec((None, tm, D), lambda b, i: (b, i, 0)),
        _resident((1, D)),
        _resident((D, N)),
        _resident((1, N)),
    ]
    args = [h, gain.reshape(1, D), w, colscale.reshape(1, N)]
    out_shape = [jax.ShapeDtypeStruct((B, S, N), jnp.bfloat16)]
    out_specs = [pl.BlockSpec((None, tm, N), lambda b, i: (b, i, 0))]
    if wt is not None:
        Nt = wt.shape[0]
        in_specs.append(_resident((Nt, D)))
        args.append(wt)
        out_shape.append(jax.ShapeDtypeStruct((B, Nt, S), jnp.bfloat16))
        out_specs.append(pl.BlockSpec((None, Nt, tm), lambda b, i: (b, 0, i)))
    res = pl.pallas_call(
        functools.partial(_proj_kernel, has_t=wt is not None),
        grid=(B, S // tm),
        in_specs=in_specs,
        out_specs=out_specs,
        out_shape=out_shape,
        compiler_params=_params(("parallel", "parallel")),
        name="norm_proj_t" if wt is not None else "norm_proj",
    )(*args)
    return res if wt is not None else res[0]


def _kvf_kernel(x_ref, g_ref, wk_ref, wvt_ref, wf_ref, bf_ref, k_ref, vt_ref, c_ref, carry_ref):
    @pl.when(pl.program_id(1) == 0)
    def _():
        carry_ref[...] = jnp.zeros_like(carry_ref)

    xn = _rms(x_ref[...], g_ref[...], RMS_EPS).astype(jnp.bfloat16)
    k_ref[...] = jnp.dot(xn, wk_ref[...], preferred_element_type=jnp.float32).astype(k_ref.dtype)
    vt_ref[...] = lax.dot_general(wvt_ref[...], xn, NT_DIMS,
                                  preferred_element_type=jnp.float32).astype(vt_ref.dtype)
    fl = jnp.dot(xn, wf_ref[...], preferred_element_type=jnp.float32) + bf_ref[...]
    ls = jnp.minimum(fl, 0.0) - jnp.log1p(jnp.exp(-jnp.abs(fl)))
    tm = ls.shape[0]
    row = lax.broadcasted_iota(jnp.int32, ls.shape, 0)
    sh = 1
    while sh < tm:
        ls = ls + jnp.where(row >= sh, pltpu.roll(ls, sh, 0), 0.0)
        sh *= 2
    c = ls + carry_ref[...]
    c_ref[...] = c
    carry_ref[...] = c[tm - 1:tm, :]


def _shared_kv(h, gain, w_kvf, b_f):
    B, S, D = h.shape
    HD = B_HEADS * B_HEAD_DIM
    tm = ROW_TILE
    wk = w_kvf[:, :HD].astype(jnp.bfloat16)
    wvt = w_kvf[:, HD:2 * HD].T.astype(jnp.bfloat16)
    wf = jnp.pad(w_kvf[:, 2 * HD:], ((0, 0), (0, LANES - B_HEADS))).astype(jnp.bfloat16)
    bf = jnp.pad(b_f, (0, LANES - B_HEADS)).reshape(1, LANES)
    return pl.pallas_call(
        _kvf_kernel,
        grid=(B, S // tm),
        in_specs=[
            pl.BlockSpec((None, tm, D), lambda b, i: (b, i, 0)),
            _resident((1, D)),
            _resident((D, HD)),
            _resident((HD, D)),
            _resident((D, LANES)),
            _resident((1, LANES)),
        ],
        out_specs=[
            pl.BlockSpec((None, tm, HD), lambda b, i: (b, i, 0)),
            pl.BlockSpec((None, HD, tm), lambda b, i: (b, 0, i)),
            pl.BlockSpec((None, tm, LANES), lambda b, i: (b, i, 0)),
        ],
        out_shape=[
            jax.ShapeDtypeStruct((B, S, HD), jnp.bfloat16),
            jax.ShapeDtypeStruct((B, HD, S), jnp.bfloat16),
            jax.ShapeDtypeStruct((B, S, LANES), jnp.float32),
        ],
        scratch_shapes=[pltpu.VMEM((1, LANES), jnp.float32)],
        compiler_params=_params(("parallel", "arbitrary")),
        name="shared_kvf",
    )(h, gain.reshape(1, D), wk, wvt, wf, bf)


def _bias_kernel(rel_ref, d_ref, bkt_ref, out_ref):
    h = pl.program_id(0)
    d = d_ref[...]
    bkt = bkt_ref[...]
    acc = jnp.zeros(d.shape, jnp.float32)
    for j in range(N_BUCKETS):
        acc = jnp.where(bkt == j, rel_ref[j, h], acc)
    far = rel_ref[N_BUCKETS - 1, h]
    out_ref[...] = jnp.where(d >= 0, (acc - far) * LOG2E, NEG_INF)


def _t5_bucket(dist):
    max_exact = N_BUCKETS // 2
    d = jnp.maximum(dist, 0)
    log_ratio = jnp.log(jnp.maximum(d, 1).astype(jnp.float32) / max_exact) / math.log(MAX_DISTANCE / max_exact)
    large = jnp.minimum(max_exact + (log_ratio * (N_BUCKETS - max_exact)).astype(jnp.int32), N_BUCKETS - 1)
    return jnp.where(d < max_exact, d, large)


def _bias_tables(rel_bias):
    TQ, TK = ATTN_TQ, ATTN_TK
    assert TQ == 2 * TK and TK >= MAX_DISTANCE
    key = jnp.arange(TK)[:, None]
    qry = jnp.arange(TQ)[None, :]
    dist = jnp.stack([TK + qry - key, qry - key, qry - key - TK]).astype(jnp.int32)
    bkt = _t5_bucket(dist).astype(jnp.int32)
    return pl.pallas_call(
        _bias_kernel,
        grid=(A_HEADS,),
        in_specs=[
            pl.BlockSpec(memory_space=pltpu.SMEM),
            _resident((3, TK, TQ)),
            _resident((3, TK, TQ)),
        ],
        out_specs=pl.BlockSpec((None, 3, TK, TQ), lambda h: (h, 0, 0, 0)),
        out_shape=jax.ShapeDtypeStruct((A_HEADS, 3, TK, TQ), jnp.float32),
        compiler_params=_params(("parallel",)),
        name="t5_bias_tiles",
    )(rel_bias, dist, bkt)


def _attn_kernel(*refs, diff, lam_init):
    if diff:
        q_ref, k_ref, vt_ref, bias_ref, lam_ref, g_ref, o_ref, *scr = refs
    else:
        q_ref, k_ref, vt_ref, c_ref, o_ref, *scr = refs
    s_slots = scr[0:2]
    streams = (scr[2:5], scr[5:8])
    TQ, TK = ATTN_TQ, ATTN_TK
    half = LANES // 2
    qi = pl.program_id(2)

    q = q_ref[...]
    lane = lax.broadcasted_iota(jnp.int32, q.shape, 1)
    zero = jnp.zeros_like(q)
    qz = (jnp.where(lane < half, q, zero), jnp.where(lane >= half, q, zero))
    for m_ref, l_ref, a_ref in streams:
        m_ref[...] = jnp.full_like(m_ref, NEG_INF)
        l_ref[...] = jnp.zeros_like(l_ref)
        a_ref[...] = jnp.zeros_like(a_ref)

    def logits(t, slot):
        off = pl.multiple_of(t * TK, TK)
        kt = k_ref[pl.ds(off, TK), :]
        for j in range(2):
            s_slots[slot][j] = lax.dot_general(kt, qz[j], NT_DIMS, preferred_element_type=jnp.float32)

    def softmax_pv(t, slot, tile_bias):
        off = pl.multiple_of(t * TK, TK)
        vt = vt_ref[:, pl.ds(off, TK)]
        if not diff:
            negc = c_ref[pl.ds(off, TK), :] * (-LOG2E)
        for j, (m_ref, l_ref, a_ref) in enumerate(streams):
            s = s_slots[slot][j]
            if not diff:
                s = s + negc[:, j:j + 1]
            if tile_bias is not None:
                s = s + tile_bias
            m_old = m_ref[...]
            m_new = jnp.maximum(m_old, jnp.max(s, axis=0, keepdims=True))
            p = jnp.exp2(s - m_new)
            alpha = jnp.exp2(m_old - m_new)
            l_ref[...] = alpha * l_ref[...] + jnp.sum(p, axis=0, keepdims=True)
            a_ref[...] = alpha * a_ref[...] + jnp.dot(vt, p.astype(jnp.bfloat16),
                                                      preferred_element_type=jnp.float32)
            m_ref[...] = m_new

    def pair(i, bias0, bias1, last):
        logits(2 * i + 1, 1)
        softmax_pv(2 * i, 0, bias0)
        if not last:
            logits(2 * i + 2, 0)
        softmax_pv(2 * i + 1, 1, bias1)

    def far_body(i, carry):
        pair(i, None, None, False)
        return carry

    logits(0, 0)
    if diff:
        lax.fori_loop(0, jnp.maximum(qi - 1, 0), far_body, 0)

        @pl.when(qi >= 1)
        def _():
            pair(qi - 1, None, bias_ref[0], False)

        pair(qi, bias_ref[1], bias_ref[2], True)
    else:
        lax.fori_loop(0, qi, far_body, 0)
        key = lax.broadcasted_iota(jnp.int32, (TK, TQ), 0)
        qry = lax.broadcasted_iota(jnp.int32, (TK, TQ), 1)
        pair(qi, jnp.where(key <= qry, 0.0, NEG_INF), jnp.where(key + TK <= qry, 0.0, NEG_INF), True)

    (_, l0_ref, a0_ref), (_, l1_ref, a1_ref) = streams
    o0 = a0_ref[...] / l0_ref[...]
    o1 = a1_ref[...] / l1_ref[...]
    if diff:
        lam = (jnp.exp(jnp.sum(lam_ref[0:1, :] * lam_ref[1:2, :], keepdims=True))
               - jnp.exp(jnp.sum(lam_ref[2:3, :] * lam_ref[3:4, :], keepdims=True))
               + lam_init)
        o = o0 - lam * o1
        ms = jnp.mean(o * o, axis=0, keepdims=True)
        o = o * lax.rsqrt(ms + SUBLN_EPS) * g_ref[...] * (1.0 - lam_init)
    else:
        row = lax.broadcasted_iota(jnp.int32, o0.shape, 0)
        o = jnp.where(row < half, o0, o1)
    o_ref[...] = o.T.astype(o_ref.dtype)


def _attn_scratch():
    TQ, TK = ATTN_TQ, ATTN_TK
    logit_slots = [pltpu.VMEM((2, TK, TQ), jnp.float32)] * 2
    one = [pltpu.VMEM((1, TQ), jnp.float32), pltpu.VMEM((1, TQ), jnp.float32),
           pltpu.VMEM((LANES, TQ), jnp.float32)]
    return logit_slots + one + one


def _diff_attention(qk, vt, bias_tab, lam_vecs, subln_g, lam_init):
    B, S, _ = qk.shape
    TQ, TK = ATTN_TQ, ATTN_TK
    H = A_HEADS
    return pl.pallas_call(
        functools.partial(_attn_kernel, diff=True, lam_init=lam_init),
        grid=(H, B, S // TQ),
        in_specs=[
            pl.BlockSpec((None, TQ, LANES), lambda h, b, i: (b, i, h)),
            pl.BlockSpec((None, S, LANES), lambda h, b, i: (b, 0, H + h)),
            pl.BlockSpec((None, LANES, S), lambda h, b, i: (b, h, 0)),
            pl.BlockSpec((None, 3, TK, TQ), lambda h, b, i: (h, 0, 0, 0)),
            _resident((4, A_HEAD_DIM)),
            _resident((LANES, 1)),
        ],
        out_specs=pl.BlockSpec((None, TQ, LANES), lambda h, b, i: (b, i, h)),
        out_shape=jax.ShapeDtypeStruct((B, S, H * LANES), jnp.bfloat16),
        scratch_shapes=_attn_scratch(),
        compiler_params=_params(("parallel", "parallel", "arbitrary")),
        name="diff_attn",
    )(qk, qk, vt, bias_tab, lam_vecs, subln_g.reshape(LANES, 1))


def _fox_attention(q, k, vt, c_pairs):
    B, S, _ = q.shape
    TQ = ATTN_TQ
    G = B_HEADS // 2
    return pl.pallas_call(
        functools.partial(_attn_kernel, diff=False, lam_init=0.0),
        grid=(G, B, S // TQ),
        in_specs=[
            pl.BlockSpec((None, TQ, LANES), lambda g, b, i: (b, i, g)),
            pl.BlockSpec((None, S, LANES), lambda g, b, i: (b, 0, g)),
            pl.BlockSpec((None, LANES, S), lambda g, b, i: (b, g, 0)),
            pl.BlockSpec((None, None, S, 2), lambda g, b, i: (b, g, 0, 0)),
        ],
        out_specs=pl.BlockSpec((None, TQ, LANES), lambda g, b, i: (b, i, g)),
        out_shape=jax.ShapeDtypeStruct((B, S, G * LANES), jnp.bfloat16),
        scratch_shapes=_attn_scratch(),
        compiler_params=_params(("parallel", "parallel", "arbitrary")),
        name="fox_attn",
    )(q, k, vt, c_pairs)


def _oproj_kernel(o_ref, w_ref, h_ref, g_ref, out_ref):
    a = jnp.dot(o_ref[...], w_ref[...], preferred_element_type=jnp.float32)
    out_ref[...] = h_ref[...] + _rms(a, g_ref[...], RMS_EPS)


def _out_project(o, w, h, gain):
    B, S, D = h.shape
    tm = ROW_TILE
    row = pl.BlockSpec((None, tm, D), lambda b, i: (b, i, 0))
    return pl.pallas_call(
        _oproj_kernel,
        grid=(B, S // tm),
        in_specs=[row, _resident((D, D)), row, _resident((1, D))],
        out_specs=row,
        out_shape=jax.ShapeDtypeStruct((B, S, D), jnp.float32),
        compiler_params=_params(("parallel", "parallel")),
        name="out_proj",
    )(o, w, h, gain.reshape(1, D))


def _ffn_kernel(h_ref, halo_ref, gpre_ref, win_ref, cw_ref, cb_ref, wout_ref, gpost_ref,
                out_ref, xe_ref, acc_ref):
    F = D_FF
    FC = FFN_CHUNK
    tm = h_ref.shape[0]
    h = h_ref[...]
    gpre = gpre_ref[...]
    first = (pl.program_id(1) == 0).astype(jnp.float32)
    xe_ref[0:HALO, :] = (_rms(halo_ref[...], gpre, RMS_EPS) * (1.0 - first)).astype(xe_ref.dtype)
    xe_ref[HALO:, :] = _rms(h, gpre, RMS_EPS).astype(xe_ref.dtype)
    acc_ref[...] = jnp.zeros_like(acc_ref)

    def conv(u, w, b):
        y = u * w[2:3, :] + pltpu.roll(u, 1, 0) * w[1:2, :] + pltpu.roll(u, 2, 0) * w[0:1, :] + b
        return y[HALO:, :]

    def chunk(c, carry):
        og = pl.multiple_of(c * FC, FC)
        ov = pl.multiple_of(F + c * FC, LANES)
        xe = xe_ref[...]
        ug = jnp.dot(xe, win_ref[:, pl.ds(og, FC)], preferred_element_type=jnp.float32)
        uv = jnp.dot(xe, win_ref[:, pl.ds(ov, FC)], preferred_element_type=jnp.float32)
        gate = conv(ug, cw_ref[:, pl.ds(og, FC)], cb_ref[:, pl.ds(og, FC)])
        val = conv(uv, cw_ref[:, pl.ds(ov, FC)], cb_ref[:, pl.ds(ov, FC)])
        act = (jax.nn.gelu(gate) * val).astype(jnp.bfloat16)
        acc_ref[...] += jnp.dot(act, wout_ref[pl.ds(og, FC), :], preferred_element_type=jnp.float32)
        return carry

    lax.fori_loop(0, F // FC, chunk, 0)
    out_ref[...] = h + _rms(acc_ref[...], gpost_ref[...], RMS_EPS)


def _conv_ffn(h, gpre, w_in, conv_w, conv_b, w_out, gpost):
    B, S, D = h.shape
    F = D_FF
    tm = ROW_TILE
    assert F % FFN_CHUNK == 0 and tm % HALO == 0
    row = pl.BlockSpec((None, tm, D), lambda b, i: (b, i, 0))
    halo = pl.BlockSpec((None, HALO, D), lambda b, i: (b, jnp.maximum(i * (tm // HALO) - 1, 0), 0))
    return pl.pallas_call(
        _ffn_kernel,
        grid=(B, S // tm),
        in_specs=[row, halo, _resident((1, D)), _resident((D, 2 * F)), _resident((3, 2 * F)),
                  _resident((1, 2 * F)), _resident((F, D)), _resident((1, D))],
        out_specs=row,
        out_shape=jax.ShapeDtypeStruct((B, S, D), jnp.float32),
        scratch_shapes=[pltpu.VMEM((tm + HALO, D), jnp.bfloat16), pltpu.VMEM((tm, D), jnp.float32)],
        compiler_params=_params(("parallel", "parallel")),
        name="conv_ffn",
    )(h, h, gpre.reshape(1, D), w_in.astype(jnp.bfloat16), conv_w, conv_b.reshape(1, 2 * F),
      w_out.astype(jnp.bfloat16), gpost.reshape(1, D))


def kernel(x, rel_bias, a_norm_pre, a_norm_post, a_w_qkv, a_lam_q1, a_lam_k1, a_lam_q2, a_lam_k2, a_subln, a_w_o, kv_norm, w_kvf, b_f, b_norm_pre, b_norm_post, b_w_q, b_w_o, ffn_norm_pre, ffn_norm_post, ffn_w_in, ffn_conv_w, ffn_conv_b, ffn_w_out):
    B, S, D = x.shape
    q_scale = LOG2E * A_HEAD_DIM ** -0.5
    nqk = 2 * D
    a_colscale = jnp.concatenate([jnp.full((D,), q_scale, jnp.float32), jnp.ones((D,), jnp.float32)])
    b_colscale = jnp.full((D,), LOG2E * B_HEAD_DIM ** -0.5, jnp.float32)
    bias_tab = _bias_tables(rel_bias)

    h = x
    for l in range(DEPTH):
        if l < N_A_LAYERS:
            w = a_w_qkv[l]
            qk, vt = _project(h, a_norm_pre[l], w[:, :nqk].astype(jnp.bfloat16), a_colscale,
                              wt=w[:, nqk:].T.astype(jnp.bfloat16))
            lam_vecs = jnp.stack([a_lam_q1[l], a_lam_k1[l], a_lam_q2[l], a_lam_k2[l]])
            o = _diff_attention(qk, vt, bias_tab, lam_vecs, a_subln[l], _lambda_init(l))
            h = _out_project(o, a_w_o[l].astype(jnp.bfloat16), h, a_norm_post[l])
        else:
            j = l - N_A_LAYERS
            if j == 0:
                k_sh, vt_sh, c_sh = _shared_kv(h, kv_norm, w_kvf, b_f)
                c_pairs = c_sh[:, :, :B_HEADS].reshape(B, S, B_HEADS // 2, 2).transpose(0, 2, 1, 3)
            q = _project(h, b_norm_pre[j], b_w_q[j].astype(jnp.bfloat16), b_colscale)
            o = _fox_attention(q, k_sh, vt_sh, c_pairs)
            h = _out_project(o, b_w_o[j].astype(jnp.bfloat16), h, b_norm_post[j])
        h = _conv_ffn(h, ffn_norm_pre[l], ffn_w_in[l], ffn_conv_w[l], ffn_conv_b[l], ffn_w_out[l],
                      ffn_norm_post[l])
    return h
```

```python
import functools
import math

import jax
import jax.numpy as jnp
from jax import lax
from jax.experimental import pallas as pl
from jax.experimental.pallas import tpu as pltpu

D_MODEL = 1024
SEQ = 4096
DEPTH = 4
N_A_LAYERS = DEPTH // 2
A_HEADS = 8
A_HEAD_DIM = 64
B_HEADS = 16
B_HEAD_DIM = 64
D_FF = 2816
N_BUCKETS = 32
MAX_DISTANCE = 128
RMS_EPS = 1e-6
SUBLN_EPS = 1e-5
NEG_INF = -1e30
LOG2E = math.log2(math.e)

LANES = 128
ATTN_TQ = 512
ATTN_TK = 256
ROW_TILE = 512
HALO = 16
ONES_ROWS = 16
VT_ROWS = LANES + ONES_ROWS
FFN_CHUNK = 256
VMEM_LIMIT = 56 * 1024 * 1024

NT_DIMS = (((1,), (1,)), ((), ()))


def _lambda_init(layer_idx):
    return 0.8 - 0.6 * math.exp(-0.3 * layer_idx)


def _rms(x, gain, eps):
    ms = jnp.mean(x * x, axis=-1, keepdims=True)
    return x * lax.rsqrt(ms + eps) * gain


def _resident(shape):
    nd = len(shape)
    return pl.BlockSpec(shape, lambda *_: (0,) * nd, pipeline_mode=pl.Buffered(1))


SCHED_FLAGS = None


def _params(sem, flags=None):
    return pltpu.CompilerParams(dimension_semantics=sem, vmem_limit_bytes=VMEM_LIMIT, flags=flags)


def _store_vt(vt_ref, yt):
    ones = jnp.ones((ONES_ROWS, yt.shape[1]), vt_ref.dtype)
    for blk in range(yt.shape[0] // LANES):
        vt_ref[blk * VT_ROWS:blk * VT_ROWS + LANES, :] = yt[blk * LANES:(blk + 1) * LANES, :].astype(vt_ref.dtype)
        vt_ref[blk * VT_ROWS + LANES:(blk + 1) * VT_ROWS, :] = ones


def _proj_kernel(x_ref, g_ref, w_ref, cs_ref, *rest, has_t):
    if has_t:
        wt_ref, o_ref, ot_ref = rest
    else:
        (o_ref,) = rest
    xn = _rms(x_ref[...], g_ref[...], RMS_EPS).astype(jnp.bfloat16)
    y = jnp.dot(xn, w_ref[...], preferred_element_type=jnp.float32)
    o_ref[...] = (y * cs_ref[...]).astype(o_ref.dtype)
    if has_t:
        _store_vt(ot_ref, lax.dot_general(wt_ref[...], xn, NT_DIMS, preferred_element_type=jnp.float32))


def _project(h, gain, w, colscale, wt=None):
    B, S, D = h.shape
    N = w.shape[1]
    tm = ROW_TILE
    in_specs = [
        pl.BlockSpec((None, tm, D), lambda b, i: (b, i, 0)),
        _resident((1, D)),
        _resident((D, N)),
        _resident((1, N)),
    ]
    args = [h, gain.reshape(1, D), w, colscale.reshape(1, N)]
    out_shape = [jax.ShapeDtypeStruct((B, S, N), jnp.bfloat16)]
    out_specs = [pl.BlockSpec((None, tm, N), lambda b, i: (b, i, 0))]
    if wt is not None:
        Nt = wt.shape[0]
        rows = Nt // LANES * VT_ROWS
        in_specs.append(_resident((Nt, D)))
        args.append(wt)
        out_shape.append(jax.ShapeDtypeStruct((B, rows, S), jnp.bfloat16))
        out_specs.append(pl.BlockSpec((None, rows, tm), lambda b, i: (b, 0, i)))
    res = pl.pallas_call(
        functools.partial(_proj_kernel, has_t=wt is not None),
        grid=(B, S // tm),
        in_specs=in_specs,
        out_specs=out_specs,
        out_shape=out_shape,
        compiler_params=_params(("parallel", "parallel")),
        name="norm_proj_t" if wt is not None else "norm_proj",
    )(*args)
    return res if wt is not None else res[0]


def _kvf_kernel(x_ref, g_ref, wk_ref, wvt_ref, wf_ref, bf_ref, k_ref, vt_ref, c_ref, carry_ref):
    @pl.when(pl.program_id(1) == 0)
    def _():
        carry_ref[...] = jnp.zeros_like(carry_ref)

    xn = _rms(x_ref[...], g_ref[...], RMS_EPS).astype(jnp.bfloat16)
    k_ref[...] = jnp.dot(xn, wk_ref[...], preferred_element_type=jnp.float32).astype(k_ref.dtype)
    _store_vt(vt_ref, lax.dot_general(wvt_ref[...], xn, NT_DIMS, preferred_element_type=jnp.float32))
    fl = jnp.dot(xn, wf_ref[...], preferred_element_type=jnp.float32) + bf_ref[...]
    ls = jnp.minimum(fl, 0.0) - jnp.log1p(jnp.exp(-jnp.abs(fl)))
    tm = ls.shape[0]
    row = lax.broadcasted_iota(jnp.int32, ls.shape, 0)
    sh = 1
    while sh < tm:
        ls = ls + jnp.where(row >= sh, pltpu.roll(ls, sh, 0), 0.0)
        sh *= 2
    c = ls + carry_ref[...]
    c_ref[...] = c
    carry_ref[...] = c[tm - 1:tm, :]


def _shared_kv(h, gain, w_kvf, b_f):
    B, S, D = h.shape
    HD = B_HEADS * B_HEAD_DIM
    tm = ROW_TILE
    wk = w_kvf[:, :HD].astype(jnp.bfloat16)
    wvt = w_kvf[:, HD:2 * HD].T.astype(jnp.bfloat16)
    wf = jnp.pad(w_kvf[:, 2 * HD:], ((0, 0), (0, LANES - B_HEADS))).astype(jnp.bfloat16)
    bf = jnp.pad(b_f, (0, LANES - B_HEADS)).reshape(1, LANES)
    return pl.pallas_call(
        _kvf_kernel,
        grid=(B, S // tm),
        in_specs=[
            pl.BlockSpec((None, tm, D), lambda b, i: (b, i, 0)),
            _resident((1, D)),
            _resident((D, HD)),
            _resident((HD, D)),
            _resident((D, LANES)),
            _resident((1, LANES)),
        ],
        out_specs=[
            pl.BlockSpec((None, tm, HD), lambda b, i: (b, i, 0)),
            pl.BlockSpec((None, HD // LANES * VT_ROWS, tm), lambda b, i: (b, 0, i)),
            pl.BlockSpec((None, tm, LANES), lambda b, i: (b, i, 0)),
        ],
        out_shape=[
            jax.ShapeDtypeStruct((B, S, HD), jnp.bfloat16),
            jax.ShapeDtypeStruct((B, HD // LANES * VT_ROWS, S), jnp.bfloat16),
            jax.ShapeDtypeStruct((B, S, LANES), jnp.float32),
        ],
        scratch_shapes=[pltpu.VMEM((1, LANES), jnp.float32)],
        compiler_params=_params(("parallel", "arbitrary")),
        name="shared_kvf",
    )(h, gain.reshape(1, D), wk, wvt, wf, bf)


def _bias_kernel(rel_ref, d_ref, bkt_ref, out_ref):
    h = pl.program_id(0)
    d = d_ref[...]
    bkt = bkt_ref[...]
    acc = jnp.zeros(d.shape, jnp.float32)
    for j in range(N_BUCKETS):
        acc = jnp.where(bkt == j, rel_ref[j, h], acc)
    far = rel_ref[N_BUCKETS - 1, h]
    out_ref[...] = jnp.where(d >= 0, (acc - far) * LOG2E, NEG_INF)


def _t5_bucket(dist):
    max_exact = N_BUCKETS // 2
    d = jnp.maximum(dist, 0)
    log_ratio = jnp.log(jnp.maximum(d, 1).astype(jnp.float32) / max_exact) / math.log(MAX_DISTANCE / max_exact)
    large = jnp.minimum(max_exact + (log_ratio * (N_BUCKETS - max_exact)).astype(jnp.int32), N_BUCKETS - 1)
    return jnp.where(d < max_exact, d, large)


def _bias_tables(rel_bias):
    TQ, TK = ATTN_TQ, ATTN_TK
    assert TQ == 2 * TK and TK >= MAX_DISTANCE
    key = jnp.arange(TK)[:, None]
    qry = jnp.arange(TQ)[None, :]
    dist = jnp.stack([TK + qry - key, qry - key, qry - key - TK]).astype(jnp.int32)
    bkt = _t5_bucket(dist).astype(jnp.int32)
    return pl.pallas_call(
        _bias_kernel,
        grid=(A_HEADS,),
        in_specs=[
            pl.BlockSpec(memory_space=pltpu.SMEM),
            _resident((3, TK, TQ)),
            _resident((3, TK, TQ)),
        ],
        out_specs=pl.BlockSpec((None, 3, TK, TQ), lambda h: (h, 0, 0, 0)),
        out_shape=jax.ShapeDtypeStruct((A_HEADS, 3, TK, TQ), jnp.float32),
        compiler_params=_params(("parallel",)),
        name="t5_bias_tiles",
    )(rel_bias, dist, bkt)


def _attn_kernel(*refs, diff, lam_init):
    if diff:
        q_ref, k_ref, vt_ref, bias_ref, lam_ref, g_ref, o_ref, *scr = refs
    else:
        q_ref, k_ref, vt_ref, c_ref, o_ref, *scr = refs
    s_slots = scr[0:2]
    streams = (scr[2:4], scr[4:6])
    TQ, TK = ATTN_TQ, ATTN_TK
    half = LANES // 2
    qi = pl.program_id(2)

    q = q_ref[...]
    lane = lax.broadcasted_iota(jnp.int32, q.shape, 1)
    zero = jnp.zeros_like(q)
    qz = (jnp.where(lane < half, q, zero), jnp.where(lane >= half, q, zero))
    for m_ref, a_ref in streams:
        m_ref[...] = jnp.full_like(m_ref, NEG_INF)
        a_ref[...] = jnp.zeros_like(a_ref)

    def logits(t, slot):
        off = pl.multiple_of(t * TK, TK)
        kt = k_ref[pl.ds(off, TK), :]
        for j in range(2):
            s_slots[slot][j] = lax.dot_general(kt, qz[j], NT_DIMS, preferred_element_type=jnp.float32)

    def softmax_pv(t, slot, tile_bias):
        off = pl.multiple_of(t * TK, TK)
        vt = vt_ref[:, pl.ds(off, TK)]
        if not diff:
            negc = c_ref[pl.ds(off, TK), :] * (-LOG2E)
        for j, (m_ref, a_ref) in enumerate(streams):
            s = s_slots[slot][j]
            if not diff:
                s = s + negc[:, j:j + 1]
            if tile_bias is not None:
                s = s + tile_bias
            m_old = m_ref[...]
            m_new = jnp.maximum(m_old, jnp.max(s, axis=0, keepdims=True))
            p = jnp.exp2(s - m_new)
            alpha = jnp.exp2(m_old - m_new)
            a_ref[...] = alpha * a_ref[...] + jnp.dot(vt, p.astype(jnp.bfloat16),
                                                      preferred_element_type=jnp.float32)
            m_ref[...] = m_new

    def pair(i, bias0, bias1, last):
        logits(2 * i + 1, 1)
        softmax_pv(2 * i, 0, bias0)
        if not last:
            logits(2 * i + 2, 0)
        softmax_pv(2 * i + 1, 1, bias1)

    def far_body(i, carry):
        pair(i, None, None, False)
        return carry

    logits(0, 0)
    if diff:
        lax.fori_loop(0, jnp.maximum(qi - 1, 0), far_body, 0)

        @pl.when(qi >= 1)
        def _():
            pair(qi - 1, None, bias_ref[0], False)

        pair(qi, bias_ref[1], bias_ref[2], True)
    else:
        lax.fori_loop(0, qi, far_body, 0)
        key = lax.broadcasted_iota(jnp.int32, (TK, TQ), 0)
        qry = lax.broadcasted_iota(jnp.int32, (TK, TQ), 1)
        pair(qi, jnp.where(key <= qry, 0.0, NEG_INF), jnp.where(key + TK <= qry, 0.0, NEG_INF), True)

    (_, a0_ref), (_, a1_ref) = streams
    o0 = a0_ref[0:LANES, :] / a0_ref[LANES:LANES + 1, :]
    o1 = a1_ref[0:LANES, :] / a1_ref[LANES:LANES + 1, :]
    if diff:
        lam = (jnp.exp(jnp.sum(lam_ref[0:1, :] * lam_ref[1:2, :], keepdims=True))
               - jnp.exp(jnp.sum(lam_ref[2:3, :] * lam_ref[3:4, :], keepdims=True))
               + lam_init)
        o = o0 - lam * o1
        ms = jnp.mean(o * o, axis=0, keepdims=True)
        o = o * lax.rsqrt(ms + SUBLN_EPS) * g_ref[...] * (1.0 - lam_init)
    else:
        row = lax.broadcasted_iota(jnp.int32, o0.shape, 0)
        o = jnp.where(row < half, o0, o1)
    o_ref[...] = o.T.astype(o_ref.dtype)


def _attn_scratch():
    TQ, TK = ATTN_TQ, ATTN_TK
    logit_slots = [pltpu.VMEM((2, TK, TQ), jnp.float32)] * 2
    one = [pltpu.VMEM((1, TQ), jnp.float32), pltpu.VMEM((VT_ROWS, TQ), jnp.float32)]
    return logit_slots + one + one


def _diff_attention(qk, vt, bias_tab, lam_vecs, subln_g, lam_init):
    B, S, _ = qk.shape
    TQ, TK = ATTN_TQ, ATTN_TK
    H = A_HEADS
    return pl.pallas_call(
        functools.partial(_attn_kernel, diff=True, lam_init=lam_init),
        grid=(H, B, S // TQ),
        in_specs=[
            pl.BlockSpec((None, TQ, LANES), lambda h, b, i: (b, i, h)),
            pl.BlockSpec((None, S, LANES), lambda h, b, i: (b, 0, H + h)),
            pl.BlockSpec((None, VT_ROWS, S), lambda h, b, i: (b, h, 0)),
            pl.BlockSpec((None, 3, TK, TQ), lambda h, b, i: (h, 0, 0, 0)),
            _resident((4, A_HEAD_DIM)),
            _resident((LANES, 1)),
        ],
        out_specs=pl.BlockSpec((None, TQ, LANES), lambda h, b, i: (b, i, h)),
        out_shape=jax.ShapeDtypeStruct((B, S, H * LANES), jnp.bfloat16),
        scratch_shapes=_attn_scratch(),
        compiler_params=_params(("parallel", "parallel", "arbitrary"), SCHED_FLAGS),
        name="diff_attn",
    )(qk, qk, vt, bias_tab, lam_vecs, subln_g.reshape(LANES, 1))


def _fox_attention(q, k, vt, c_pairs):
    B, S, _ = q.shape
    TQ = ATTN_TQ
    G = B_HEADS // 2
    return pl.pallas_call(
        functools.partial(_attn_kernel, diff=False, lam_init=0.0),
        grid=(G, B, S // TQ),
        in_specs=[
            pl.BlockSpec((None, TQ, LANES), lambda g, b, i: (b, i, g)),
            pl.BlockSpec((None, S, LANES), lambda g, b, i: (b, 0, g)),
            pl.BlockSpec((None, VT_ROWS, S), lambda g, b, i: (b, g, 0)),
            pl.BlockSpec((None, None, S, 2), lambda g, b, i: (b, g, 0, 0)),
        ],
        out_specs=pl.BlockSpec((None, TQ, LANES), lambda g, b, i: (b, i, g)),
        out_shape=jax.ShapeDtypeStruct((B, S, G * LANES), jnp.bfloat16),
        scratch_shapes=_attn_scratch(),
        compiler_params=_params(("parallel", "parallel", "arbitrary"), SCHED_FLAGS),
        name="fox_attn",
    )(q, k, vt, c_pairs)


def _oproj_kernel(o_ref, w_ref, h_ref, g_ref, out_ref):
    a = jnp.dot(o_ref[...], w_ref[...], preferred_element_type=jnp.float32)
    out_ref[...] = h_ref[...] + _rms(a, g_ref[...], RMS_EPS)


def _out_project(o, w, h, gain):
    B, S, D = h.shape
    tm = ROW_TILE
    row = pl.BlockSpec((None, tm, D), lambda b, i: (b, i, 0))
    return pl.pallas_call(
        _oproj_kernel,
        grid=(B, S // tm),
        in_specs=[row, _resident((D, D)), row, _resident((1, D))],
        out_specs=row,
        out_shape=jax.ShapeDtypeStruct((B, S, D), jnp.float32),
        compiler_params=_params(("parallel", "parallel")),
        name="out_proj",
    )(o, w, h, gain.reshape(1, D))


def _ffn_kernel(h_ref, halo_ref, gpre_ref, win_ref, cw_ref, cb_ref, wout_ref, gpost_ref,
                out_ref, xe_ref, acc_ref, u0_ref, u1_ref):
    F = D_FF
    FC = FFN_CHUNK
    n_chunks = F // FC
    u_slots = (u0_ref, u1_ref)
    h = h_ref[...]
    gpre = gpre_ref[...]
    first = (pl.program_id(1) == 0).astype(jnp.float32)
    xe_ref[0:HALO, :] = (_rms(halo_ref[...], gpre, RMS_EPS) * (1.0 - first)).astype(xe_ref.dtype)
    xe_ref[HALO:, :] = _rms(h, gpre, RMS_EPS).astype(xe_ref.dtype)
    acc_ref[...] = jnp.zeros_like(acc_ref)

    def conv(u_ref, w, b):
        tm = u_ref.shape[0] - HALO
        return (u_ref[pl.ds(HALO, tm), :] * w[2:3, :] + u_ref[pl.ds(HALO - 1, tm), :] * w[1:2, :]
                + u_ref[pl.ds(HALO - 2, tm), :] * w[0:1, :] + b)

    def offsets(c):
        return pl.multiple_of(c * FC, FC), pl.multiple_of(F + c * FC, LANES)

    def up_proj(c, slot):
        og, ov = offsets(c)
        xe = xe_ref[...]
        u_slots[slot][0] = jnp.dot(xe, win_ref[:, pl.ds(og, FC)], preferred_element_type=jnp.float32)
        u_slots[slot][1] = jnp.dot(xe, win_ref[:, pl.ds(ov, FC)], preferred_element_type=jnp.float32)

    def conv_act_down(c, slot):
        og, ov = offsets(c)
        gate = conv(u_slots[slot].at[0], cw_ref[:, pl.ds(og, FC)], cb_ref[:, pl.ds(og, FC)])
        val = conv(u_slots[slot].at[1], cw_ref[:, pl.ds(ov, FC)], cb_ref[:, pl.ds(ov, FC)])
        act = (jax.nn.gelu(gate) * val).astype(jnp.bfloat16)
        acc_ref[...] += jnp.dot(act, wout_ref[pl.ds(og, FC), :], preferred_element_type=jnp.float32)

    def pair(i, carry):
        up_proj(2 * i + 1, 1)
        conv_act_down(2 * i, 0)
        up_proj(2 * i + 2, 0)
        conv_act_down(2 * i + 1, 1)
        return carry

    assert n_chunks % 2 == 1
    up_proj(0, 0)
    lax.fori_loop(0, n_chunks // 2, pair, 0)
    conv_act_down(n_chunks - 1, 0)
    out_ref[...] = h + _rms(acc_ref[...], gpost_ref[...], RMS_EPS)


def _conv_ffn(h, gpre, w_in, conv_w, conv_b, w_out, gpost):
    B, S, D = h.shape
    F = D_FF
    tm = ROW_TILE
    assert F % FFN_CHUNK == 0 and tm % HALO == 0
    row = pl.BlockSpec((None, tm, D), lambda b, i: (b, i, 0))
    halo = pl.BlockSpec((None, HALO, D), lambda b, i: (b, jnp.maximum(i * (tm // HALO) - 1, 0), 0))
    return pl.pallas_call(
        _ffn_kernel,
        grid=(B, S // tm),
        in_specs=[row, halo, _resident((1, D)), _resident((D, 2 * F)), _resident((3, 2 * F)),
                  _resident((1, 2 * F)), _resident((F, D)), _resident((1, D))],
        out_specs=row,
        out_shape=jax.ShapeDtypeStruct((B, S, D), jnp.float32),
        scratch_shapes=[pltpu.VMEM((tm + HALO, D), jnp.bfloat16), pltpu.VMEM((tm, D), jnp.float32)]
        + [pltpu.VMEM((2, tm + HALO, FFN_CHUNK), jnp.float32)] * 2,
        compiler_params=_params(("parallel", "parallel"), SCHED_FLAGS),
        name="conv_ffn",
    )(h, h, gpre.reshape(1, D), w_in.astype(jnp.bfloat16), conv_w, conv_b.reshape(1, 2 * F),
      w_out.astype(jnp.bfloat16), gpost.reshape(1, D))


def kernel(x, rel_bias, a_norm_pre, a_norm_post, a_w_qkv, a_lam_q1, a_lam_k1, a_lam_q2, a_lam_k2, a_subln, a_w_o, kv_norm, w_kvf, b_f, b_norm_pre, b_norm_post, b_w_q, b_w_o, ffn_norm_pre, ffn_norm_post, ffn_w_in, ffn_conv_w, ffn_conv_b, ffn_w_out):
    B, S, D = x.shape
    q_scale = LOG2E * A_HEAD_DIM ** -0.5
    nqk = 2 * D
    a_colscale = jnp.concatenate([jnp.full((D,), q_scale, jnp.float32), jnp.ones((D,), jnp.float32)])
    b_colscale = jnp.full((D,), LOG2E * B_HEAD_DIM ** -0.5, jnp.float32)
    bias_tab = _bias_tables(rel_bias)

    h = x
    for l in range(DEPTH):
        if l < N_A_LAYERS:
            w = a_w_qkv[l]
            qk, vt = _project(h, a_norm_pre[l], w[:, :nqk].astype(jnp.bfloat16), a_colscale,
                              wt=w[:, nqk:].T.astype(jnp.bfloat16))
            lam_vecs = jnp.stack([a_lam_q1[l], a_lam_k1[l], a_lam_q2[l], a_lam_k2[l]])
            o = _diff_attention(qk, vt, bias_tab, lam_vecs, a_subln[l], _lambda_init(l))
            h = _out_project(o, a_w_o[l].astype(jnp.bfloat16), h, a_norm_post[l])
        else:
            j = l - N_A_LAYERS
            if j == 0:
                k_sh, vt_sh, c_sh = _shared_kv(h, kv_norm, w_kvf, b_f)
                c_pairs = c_sh[:, :, :B_HEADS].reshape(B, S, B_HEADS // 2, 2).transpose(0, 2, 1, 3)
            q = _project(h, b_norm_pre[j], b_w_q[j].astype(jnp.bfloat16), b_colscale)
            o = _fox_attention(q, k_sh, vt_sh, c_pairs)
            h = _out_project(o, b_w_o[j].astype(jnp.bfloat16), h, b_norm_post[j])
        h = _conv_ffn(h, ffn_norm_pre[l], ffn_w_in[l], ffn_conv_w[l], ffn_conv_b[l], ffn_w_out[l],
                      ffn_norm_post[l])
    return h
```

```python
import functools
import math

import jax
import jax.numpy as jnp
from jax import lax
from jax.experimental import pallas as pl
from jax.experimental.pallas import tpu as pltpu

D_MODEL = 1024
SEQ = 4096
DEPTH = 4
N_A_LAYERS = DEPTH // 2
A_HEADS = 8
A_HEAD_DIM = 64
B_HEADS = 16
B_HEAD_DIM = 64
D_FF = 2816
N_BUCKETS = 32
MAX_DISTANCE = 128
RMS_EPS = 1e-6
SUBLN_EPS = 1e-5
NEG_INF = -1e30
LOG2E = math.log2(math.e)

LANES = 128
ATTN_TQ = 512
ATTN_TK = 256
ROW_TILE = 512
HALO = 16
ONES_ROWS = 16
VT_ROWS = LANES + ONES_ROWS
FFN_CHUNK = 256
VMEM_LIMIT = 56 * 1024 * 1024

NT_DIMS = (((1,), (1,)), ((), ()))


def _lambda_init(layer_idx):
    return 0.8 - 0.6 * math.exp(-0.3 * layer_idx)


def _rms(x, gain, eps):
    ms = jnp.mean(x * x, axis=-1, keepdims=True)
    return x * lax.rsqrt(ms + eps) * gain


def _resident(shape):
    nd = len(shape)
    return pl.BlockSpec(shape, lambda *_: (0,) * nd, pipeline_mode=pl.Buffered(1))


SCHED_FLAGS = None


def _params(sem, flags=None):
    return pltpu.CompilerParams(dimension_semantics=sem, vmem_limit_bytes=VMEM_LIMIT, flags=flags)


def _store_vt(vt_ref, yt):
    ones = jnp.ones((ONES_ROWS, yt.shape[1]), vt_ref.dtype)
    for blk in range(yt.shape[0] // LANES):
        vt_ref[blk * VT_ROWS:blk * VT_ROWS + LANES, :] = yt[blk * LANES:(blk + 1) * LANES, :].astype(vt_ref.dtype)
        vt_ref[blk * VT_ROWS + LANES:(blk + 1) * VT_ROWS, :] = ones


def _proj_kernel(x_ref, g_ref, w_ref, cs_ref, *rest, has_t):
    if has_t:
        wt_ref, o_ref, ot_ref = rest
    else:
        (o_ref,) = rest
    xn = _rms(x_ref[...], g_ref[...], RMS_EPS).astype(jnp.bfloat16)
    y = jnp.dot(xn, w_ref[...], preferred_element_type=jnp.float32)
    o_ref[...] = (y * cs_ref[...]).astype(o_ref.dtype)
    if has_t:
        _store_vt(ot_ref, lax.dot_general(wt_ref[...], xn, NT_DIMS, preferred_element_type=jnp.float32))


def _project(h, gain, w, colscale, wt=None):
    B, S, D = h.shape
    N = w.shape[1]
    tm = ROW_TILE
    in_specs = [
        pl.BlockSpec((None, tm, D), lambda b, i: (b, i, 0)),
        _resident((1, D)),
        _resident((D, N)),
        _resident((1, N)),
    ]
    args = [h, gain.reshape(1, D), w, colscale.reshape(1, N)]
    out_shape = [jax.ShapeDtypeStruct((B, S, N), jnp.bfloat16)]
    out_specs = [pl.BlockSpec((None, tm, N), lambda b, i: (b, i, 0))]
    if wt is not None:
        Nt = wt.shape[0]
        rows = Nt // LANES * VT_ROWS
        in_specs.append(_resident((Nt, D)))
        args.append(wt)
        out_shape.append(jax.ShapeDtypeStruct((B, rows, S), jnp.bfloat16))
        out_specs.append(pl.BlockSpec((None, rows, tm), lambda b, i: (b, 0, i)))
    res = pl.pallas_call(
        functools.partial(_proj_kernel, has_t=wt is not None),
        grid=(B, S // tm),
        in_specs=in_specs,
        out_specs=out_specs,
        out_shape=out_shape,
        compiler_params=_params(("parallel", "parallel")),
        name="norm_proj_t" if wt is not None else "norm_proj",
    )(*args)
    return res if wt is not None else res[0]


def _kvf_kernel(x_ref, g_ref, wk_ref, wvt_ref, wf_ref, bf_ref, k_ref, vt_ref, c_ref, carry_ref):
    @pl.when(pl.program_id(1) == 0)
    def _():
        carry_ref[...] = jnp.zeros_like(carry_ref)

    xn = _rms(x_ref[...], g_ref[...], RMS_EPS).astype(jnp.bfloat16)
    k_ref[...] = jnp.dot(xn, wk_ref[...], preferred_element_type=jnp.float32).astype(k_ref.dtype)
    _store_vt(vt_ref, lax.dot_general(wvt_ref[...], xn, NT_DIMS, preferred_element_type=jnp.float32))
    fl = jnp.dot(xn, wf_ref[...], preferred_element_type=jnp.float32) + bf_ref[...]
    ls = jnp.minimum(fl, 0.0) - jnp.log1p(jnp.exp(-jnp.abs(fl)))
    tm = ls.shape[0]
    row = lax.broadcasted_iota(jnp.int32, ls.shape, 0)
    sh = 1
    while sh < tm:
        ls = ls + jnp.where(row >= sh, pltpu.roll(ls, sh, 0), 0.0)
        sh *= 2
    c = ls + carry_ref[...]
    c_ref[...] = c
    carry_ref[...] = c[tm - 1:tm, :]


def _shared_kv(h, gain, w_kvf, b_f):
    B, S, D = h.shape
    HD = B_HEADS * B_HEAD_DIM
    tm = ROW_TILE
    wk = w_kvf[:, :HD].astype(jnp.bfloat16)
    wvt = w_kvf[:, HD:2 * HD].T.astype(jnp.bfloat16)
    wf = jnp.pad(w_kvf[:, 2 * HD:], ((0, 0), (0, LANES - B_HEADS))).astype(jnp.bfloat16)
    bf = jnp.pad(b_f, (0, LANES - B_HEADS)).reshape(1, LANES)
    return pl.pallas_call(
        _kvf_kernel,
        grid=(B, S // tm),
        in_specs=[
            pl.BlockSpec((None, tm, D), lambda b, i: (b, i, 0)),
            _resident((1, D)),
            _resident((D, HD)),
            _resident((HD, D)),
            _resident((D, LANES)),
            _resident((1, LANES)),
        ],
        out_specs=[
            pl.BlockSpec((None, tm, HD), lambda b, i: (b, i, 0)),
            pl.BlockSpec((None, HD // LANES * VT_ROWS, tm), lambda b, i: (b, 0, i)),
            pl.BlockSpec((None, tm, LANES), lambda b, i: (b, i, 0)),
        ],
        out_shape=[
            jax.ShapeDtypeStruct((B, S, HD), jnp.bfloat16),
            jax.ShapeDtypeStruct((B, HD // LANES * VT_ROWS, S), jnp.bfloat16),
            jax.ShapeDtypeStruct((B, S, LANES), jnp.float32),
        ],
        scratch_shapes=[pltpu.VMEM((1, LANES), jnp.float32)],
        compiler_params=_params(("parallel", "arbitrary")),
        name="shared_kvf",
    )(h, gain.reshape(1, D), wk, wvt, wf, bf)


def _bias_kernel(rel_ref, d_ref, bkt_ref, out_ref):
    h = pl.program_id(0)
    d = d_ref[...]
    bkt = bkt_ref[...]
    acc = jnp.zeros(d.shape, jnp.float32)
    for j in range(N_BUCKETS):
        acc = jnp.where(bkt == j, rel_ref[j, h], acc)
    far = rel_ref[N_BUCKETS - 1, h]
    out_ref[...] = jnp.where(d >= 0, (acc - far) * LOG2E, NEG_INF)


def _t5_bucket(dist):
    max_exact = N_BUCKETS // 2
    d = jnp.maximum(dist, 0)
    log_ratio = jnp.log(jnp.maximum(d, 1).astype(jnp.float32) / max_exact) / math.log(MAX_DISTANCE / max_exact)
    large = jnp.minimum(max_exact + (log_ratio * (N_BUCKETS - max_exact)).astype(jnp.int32), N_BUCKETS - 1)
    return jnp.where(d < max_exact, d, large)


def _bias_tables(rel_bias):
    TQ, TK = ATTN_TQ, ATTN_TK
    assert TQ == 2 * TK and TK >= MAX_DISTANCE
    key = jnp.arange(TK)[:, None]
    qry = jnp.arange(TQ)[None, :]
    dist = jnp.stack([TK + qry - key, qry - key, qry - key - TK]).astype(jnp.int32)
    bkt = _t5_bucket(dist).astype(jnp.int32)
    return pl.pallas_call(
        _bias_kernel,
        grid=(A_HEADS,),
        in_specs=[
            pl.BlockSpec(memory_space=pltpu.SMEM),
            _resident((3, TK, TQ)),
            _resident((3, TK, TQ)),
        ],
        out_specs=pl.BlockSpec((None, 3, TK, TQ), lambda h: (h, 0, 0, 0)),
        out_shape=jax.ShapeDtypeStruct((A_HEADS, 3, TK, TQ), jnp.float32),
        compiler_params=_params(("parallel",)),
        name="t5_bias_tiles",
    )(rel_bias, dist, bkt)


def _attn_kernel(*refs, diff, lam_init):
    if diff:
        q_ref, k_ref, vt_ref, bias_ref, lam_ref, g_ref, o_ref, *scr = refs
    else:
        q_ref, k_ref, vt_ref, c_ref, o_ref, *scr = refs
    s_slots = scr[0:2]
    streams = (scr[2:4], scr[4:6])
    TQ, TK = ATTN_TQ, ATTN_TK
    half = LANES // 2
    qi = pl.program_id(2)

    q = q_ref[...]
    lane = lax.broadcasted_iota(jnp.int32, q.shape, 1)
    zero = jnp.zeros_like(q)
    qz = (jnp.where(lane < half, q, zero), jnp.where(lane >= half, q, zero))
    if not diff:
        key = lax.broadcasted_iota(jnp.int32, (TK, TQ), 0)
        qry = lax.broadcasted_iota(jnp.int32, (TK, TQ), 1)
        diag_masks = (jnp.where(key <= qry, 0.0, NEG_INF), jnp.where(key + TK <= qry, 0.0, NEG_INF))

    def logits(t, slot):
        kt = k_ref[t * TK:(t + 1) * TK, :]
        for j in range(2):
            s_slots[slot][j] = lax.dot_general(kt, qz[j], NT_DIMS, preferred_element_type=jnp.float32)

    def softmax_pv(t, slot, tile_bias):
        vt = vt_ref[:, t * TK:(t + 1) * TK]
        if not diff:
            negc = c_ref[t * TK:(t + 1) * TK, :] * (-LOG2E)
        for j, (m_ref, a_ref) in enumerate(streams):
            s = s_slots[slot][j]
            if not diff:
                s = s + negc[:, j:j + 1]
            if tile_bias is not None:
                s = s + tile_bias
            m_new = jnp.max(s, axis=0, keepdims=True)
            if t > 0:
                m_old = m_ref[...]
                m_new = jnp.maximum(m_old, m_new)
            pv = jnp.dot(vt, jnp.exp2((s - m_new).astype(jnp.bfloat16)), preferred_element_type=jnp.float32)
            a_ref[...] = pv if t == 0 else jnp.exp2(m_old - m_new) * a_ref[...] + pv
            m_ref[...] = m_new

    def walk(v):
        n = 2 * v + 2
        logits(0, 0)
        for t in range(n):
            if t + 1 < n:
                logits(t + 1, (t + 1) % 2)
            from_end = n - 1 - t
            bias = None
            if diff and from_end < 3:
                bias = bias_ref[2 - from_end]
            elif not diff and from_end < 2:
                bias = diag_masks[1 - from_end]
            softmax_pv(t, t % 2, bias)

    for v in range(k_ref.shape[0] // TQ):
        pl.when(qi == v)(functools.partial(walk, v))

    (_, a0_ref), (_, a1_ref) = streams
    o0 = a0_ref[0:LANES, :] / a0_ref[LANES:LANES + 1, :]
    o1 = a1_ref[0:LANES, :] / a1_ref[LANES:LANES + 1, :]
    if diff:
        lam = (jnp.exp(jnp.sum(lam_ref[0:1, :] * lam_ref[1:2, :], keepdims=True))
               - jnp.exp(jnp.sum(lam_ref[2:3, :] * lam_ref[3:4, :], keepdims=True))
               + lam_init)
        o = o0 - lam * o1
        ms = jnp.mean(o * o, axis=0, keepdims=True)
        o = o * lax.rsqrt(ms + SUBLN_EPS) * g_ref[...] * (1.0 - lam_init)
    else:
        row = lax.broadcasted_iota(jnp.int32, o0.shape, 0)
        o = jnp.where(row < half, o0, o1)
    o_ref[...] = o.T.astype(o_ref.dtype)


def _attn_scratch():
    TQ, TK = ATTN_TQ, ATTN_TK
    logit_slots = [pltpu.VMEM((2, TK, TQ), jnp.float32)] * 2
    one = [pltpu.VMEM((1, TQ), jnp.float32), pltpu.VMEM((VT_ROWS, TQ), jnp.float32)]
    return logit_slots + one + one


def _diff_attention(qk, vt, bias_tab, lam_vecs, subln_g, lam_init):
    B, S, _ = qk.shape
    TQ, TK = ATTN_TQ, ATTN_TK
    H = A_HEADS
    return pl.pallas_call(
        functools.partial(_attn_kernel, diff=True, lam_init=lam_init),
        grid=(H, B, S // TQ),
        in_specs=[
            pl.BlockSpec((None, TQ, LANES), lambda h, b, i: (b, i, h)),
            pl.BlockSpec((None, S, LANES), lambda h, b, i: (b, 0, H + h)),
            pl.BlockSpec((None, VT_ROWS, S), lambda h, b, i: (b, h, 0)),
            pl.BlockSpec((None, 3, TK, TQ), lambda h, b, i: (h, 0, 0, 0)),
            _resident((4, A_HEAD_DIM)),
            _resident((LANES, 1)),
        ],
        out_specs=pl.BlockSpec((None, TQ, LANES), lambda h, b, i: (b, i, h)),
        out_shape=jax.ShapeDtypeStruct((B, S, H * LANES), jnp.bfloat16),
        scratch_shapes=_attn_scratch(),
        compiler_params=_params(("parallel", "parallel", "arbitrary"), SCHED_FLAGS),
        name="diff_attn",
    )(qk, qk, vt, bias_tab, lam_vecs, subln_g.reshape(LANES, 1))


def _fox_attention(q, k, vt, c_pairs):
    B, S, _ = q.shape
    TQ = ATTN_TQ
    G = B_HEADS // 2
    return pl.pallas_call(
        functools.partial(_attn_kernel, diff=False, lam_init=0.0),
        grid=(G, B, S // TQ),
        in_specs=[
            pl.BlockSpec((None, TQ, LANES), lambda g, b, i: (b, i, g)),
            pl.BlockSpec((None, S, LANES), lambda g, b, i: (b, 0, g)),
            pl.BlockSpec((None, VT_ROWS, S), lambda g, b, i: (b, g, 0)),
            pl.BlockSpec((None, None, S, 2), lambda g, b, i: (b, g, 0, 0)),
        ],
        out_specs=pl.BlockSpec((None, TQ, LANES), lambda g, b, i: (b, i, g)),
        out_shape=jax.ShapeDtypeStruct((B, S, G * LANES), jnp.bfloat16),
        scratch_shapes=_attn_scratch(),
        compiler_params=_params(("parallel", "parallel", "arbitrary"), SCHED_FLAGS),
        name="fox_attn",
    )(q, k, vt, c_pairs)


def _oproj_kernel(o_ref, w_ref, h_ref, g_ref, out_ref):
    a = jnp.dot(o_ref[...], w_ref[...], preferred_element_type=jnp.float32)
    out_ref[...] = h_ref[...] + _rms(a, g_ref[...], RMS_EPS)


def _out_project(o, w, h, gain):
    B, S, D = h.shape
    tm = ROW_TILE
    row = pl.BlockSpec((None, tm, D), lambda b, i: (b, i, 0))
    return pl.pallas_call(
        _oproj_kernel,
        grid=(B, S // tm),
        in_specs=[row, _resident((D, D)), row, _resident((1, D))],
        out_specs=row,
        out_shape=jax.ShapeDtypeStruct((B, S, D), jnp.float32),
        compiler_params=_params(("parallel", "parallel")),
        name="out_proj",
    )(o, w, h, gain.reshape(1, D))


def _ffn_kernel(h_ref, halo_ref, gpre_ref, win_ref, cw_ref, cb_ref, wout_ref, gpost_ref,
                out_ref, xe_ref, acc_ref, u0_ref, u1_ref):
    F = D_FF
    FC = FFN_CHUNK
    n_chunks = F // FC
    u_slots = (u0_ref, u1_ref)
    h = h_ref[...]
    gpre = gpre_ref[...]
    first = (pl.program_id(1) == 0).astype(jnp.float32)
    xe_ref[0:HALO, :] = (_rms(halo_ref[...], gpre, RMS_EPS) * (1.0 - first)).astype(xe_ref.dtype)
    xe_ref[HALO:, :] = _rms(h, gpre, RMS_EPS).astype(xe_ref.dtype)
    acc_ref[...] = jnp.zeros_like(acc_ref)

    def conv(u_ref, w, b):
        tm = u_ref.shape[0] - HALO
        return (u_ref[pl.ds(HALO, tm), :] * w[2:3, :] + u_ref[pl.ds(HALO - 1, tm), :] * w[1:2, :]
                + u_ref[pl.ds(HALO - 2, tm), :] * w[0:1, :] + b)

    def offsets(c):
        return pl.multiple_of(c * FC, FC), pl.multiple_of(F + c * FC, LANES)

    def up_proj(c, slot):
        og, ov = offsets(c)
        xe = xe_ref[...]
        u_slots[slot][0] = jnp.dot(xe, win_ref[:, pl.ds(og, FC)], preferred_element_type=jnp.float32)
        u_slots[slot][1] = jnp.dot(xe, win_ref[:, pl.ds(ov, FC)], preferred_element_type=jnp.float32)

    def conv_act_down(c, slot):
        og, ov = offsets(c)
        gate = conv(u_slots[slot].at[0], cw_ref[:, pl.ds(og, FC)], cb_ref[:, pl.ds(og, FC)])
        val = conv(u_slots[slot].at[1], cw_ref[:, pl.ds(ov, FC)], cb_ref[:, pl.ds(ov, FC)])
        act = (jax.nn.gelu(gate) * val).astype(jnp.bfloat16)
        acc_ref[...] += jnp.dot(act, wout_ref[pl.ds(og, FC), :], preferred_element_type=jnp.float32)

    def pair(i, carry):
        up_proj(2 * i + 1, 1)
        conv_act_down(2 * i, 0)
        up_proj(2 * i + 2, 0)
        conv_act_down(2 * i + 1, 1)
        return carry

    assert n_chunks % 2 == 1
    up_proj(0, 0)
    lax.fori_loop(0, n_chunks // 2, pair, 0)
    conv_act_down(n_chunks - 1, 0)
    out_ref[...] = h + _rms(acc_ref[...], gpost_ref[...], RMS_EPS)


def _conv_ffn(h, gpre, w_in, conv_w, conv_b, w_out, gpost):
    B, S, D = h.shape
    F = D_FF
    tm = ROW_TILE
    assert F % FFN_CHUNK == 0 and tm % HALO == 0
    row = pl.BlockSpec((None, tm, D), lambda b, i: (b, i, 0))
    halo = pl.BlockSpec((None, HALO, D), lambda b, i: (b, jnp.maximum(i * (tm // HALO) - 1, 0), 0))
    return pl.pallas_call(
        _ffn_kernel,
        grid=(B, S // tm),
        in_specs=[row, halo, _resident((1, D)), _resident((D, 2 * F)), _resident((3, 2 * F)),
                  _resident((1, 2 * F)), _resident((F, D)), _resident((1, D))],
        out_specs=row,
        out_shape=jax.ShapeDtypeStruct((B, S, D), jnp.float32),
        scratch_shapes=[pltpu.VMEM((tm + HALO, D), jnp.bfloat16), pltpu.VMEM((tm, D), jnp.float32)]
        + [pltpu.VMEM((2, tm + HALO, FFN_CHUNK), jnp.float32)] * 2,
        compiler_params=_params(("parallel", "parallel"), SCHED_FLAGS),
        name="conv_ffn",
    )(h, h, gpre.reshape(1, D), w_in.astype(jnp.bfloat16), conv_w, conv_b.reshape(1, 2 * F),
      w_out.astype(jnp.bfloat16), gpost.reshape(1, D))


def kernel(x, rel_bias, a_norm_pre, a_norm_post, a_w_qkv, a_lam_q1, a_lam_k1, a_lam_q2, a_lam_k2, a_subln, a_w_o, kv_norm, w_kvf, b_f, b_norm_pre, b_norm_post, b_w_q, b_w_o, ffn_norm_pre, ffn_norm_post, ffn_w_in, ffn_conv_w, ffn_conv_b, ffn_w_out):
    B, S, D = x.shape
    q_scale = LOG2E * A_HEAD_DIM ** -0.5
    nqk = 2 * D
    a_colscale = jnp.concatenate([jnp.full((D,), q_scale, jnp.float32), jnp.ones((D,), jnp.float32)])
    b_colscale = jnp.full((D,), LOG2E * B_HEAD_DIM ** -0.5, jnp.float32)
    bias_tab = _bias_tables(rel_bias)

    h = x
    for l in range(DEPTH):
        if l < N_A_LAYERS:
            w = a_w_qkv[l]
            qk, vt = _project(h, a_norm_pre[l], w[:, :nqk].astype(jnp.bfloat16), a_colscale,
                              wt=w[:, nqk:].T.astype(jnp.bfloat16))
            lam_vecs = jnp.stack([a_lam_q1[l], a_lam_k1[l], a_lam_q2[l], a_lam_k2[l]])
            o = _diff_attention(qk, vt, bias_tab, lam_vecs, a_subln[l], _lambda_init(l))
            h = _out_project(o, a_w_o[l].astype(jnp.bfloat16), h, a_norm_post[l])
        else:
            j = l - N_A_LAYERS
            if j == 0:
                k_sh, vt_sh, c_sh = _shared_kv(h, kv_norm, w_kvf, b_f)
                c_pairs = c_sh[:, :, :B_HEADS].reshape(B, S, B_HEADS // 2, 2).transpose(0, 2, 1, 3)
            q = _project(h, b_norm_pre[j], b_w_q[j].astype(jnp.bfloat16), b_colscale)
            o = _fox_attention(q, k_sh, vt_sh, c_pairs)
            h = _out_project(o, b_w_o[j].astype(jnp.bfloat16), h, b_norm_post[j])
        h = _conv_ffn(h, ffn_norm_pre[l], ffn_w_in[l], ffn_conv_w[l], ffn_conv_b[l], ffn_w_out[l],
                      ffn_norm_post[l])
    return h
```

```python
import functools
import math

import jax
import jax.numpy as jnp
import numpy as np
from jax import lax
from jax.experimental import pallas as pl
from jax.experimental.pallas import tpu as pltpu

D_MODEL = 1024
SEQ = 4096
DEPTH = 4
N_A_LAYERS = DEPTH // 2
A_HEADS = 8
A_HEAD_DIM = 64
B_HEADS = 16
B_HEAD_DIM = 64
D_FF = 2816
N_BUCKETS = 32
MAX_DISTANCE = 128
RMS_EPS = 1e-6
SUBLN_EPS = 1e-5
NEG_INF = -1e30
LOG2E = math.log2(math.e)

LANES = 128
ATTN_T = 512
ROW_TILE = 512
HALO = 16
ONES_ROWS = 16
VT_ROWS = LANES + ONES_ROWS
DECAY_LIMBS = 3
FFN_CHUNK = 256
VMEM_LIMIT = 56 * 1024 * 1024

NT_DIMS = (((1,), (1,)), ((), ()))


def _lambda_init(layer_idx):
    return 0.8 - 0.6 * math.exp(-0.3 * layer_idx)


def _rms(x, gain, eps):
    ms = jnp.mean(x * x, axis=-1, keepdims=True)
    return x * lax.rsqrt(ms + eps) * gain


def _resident(shape):
    nd = len(shape)
    return pl.BlockSpec(shape, lambda *_: (0,) * nd, pipeline_mode=pl.Buffered(1))


def _params(sem):
    return pltpu.CompilerParams(dimension_semantics=sem, vmem_limit_bytes=VMEM_LIMIT)


def _store_vt(vt_ref, yt):
    ones = jnp.ones((ONES_ROWS, yt.shape[1]), vt_ref.dtype)
    for blk in range(yt.shape[0] // LANES):
        vt_ref[blk * VT_ROWS:blk * VT_ROWS + LANES, :] = yt[blk * LANES:(blk + 1) * LANES, :].astype(vt_ref.dtype)
        vt_ref[blk * VT_ROWS + LANES:(blk + 1) * VT_ROWS, :] = ones


def _proj_kernel(x_ref, g_ref, w_ref, cs_ref, *rest, has_t):
    if has_t:
        wt_ref, o_ref, ot_ref = rest
    else:
        (o_ref,) = rest
    xn = _rms(x_ref[...], g_ref[...], RMS_EPS).astype(jnp.bfloat16)
    y = jnp.dot(xn, w_ref[...], preferred_element_type=jnp.float32)
    o_ref[...] = (y * cs_ref[...]).astype(o_ref.dtype)
    if has_t:
        _store_vt(ot_ref, lax.dot_general(wt_ref[...], xn, NT_DIMS, preferred_element_type=jnp.float32))


def _project(h, gain, w, colscale, wt=None):
    B, S, D = h.shape
    N = w.shape[1]
    tm = ROW_TILE
    in_specs = [
        pl.BlockSpec((None, tm, D), lambda b, i: (b, i, 0)),
        _resident((1, D)),
        _resident((D, N)),
        _resident((1, N)),
    ]
    args = [h, gain.reshape(1, D), w, colscale.reshape(1, N)]
    out_shape = [jax.ShapeDtypeStruct((B, S, N), jnp.bfloat16)]
    out_specs = [pl.BlockSpec((None, tm, N), lambda b, i: (b, i, 0))]
    if wt is not None:
        Nt = wt.shape[0]
        rows = Nt // LANES * VT_ROWS
        in_specs.append(_resident((Nt, D)))
        args.append(wt)
        out_shape.append(jax.ShapeDtypeStruct((B, rows, S), jnp.bfloat16))
        out_specs.append(pl.BlockSpec((None, rows, tm), lambda b, i: (b, 0, i)))
    res = pl.pallas_call(
        functools.partial(_proj_kernel, has_t=wt is not None),
        grid=(B, S // tm),
        in_specs=in_specs,
        out_specs=out_specs,
        out_shape=out_shape,
        compiler_params=_params(("parallel", "parallel")),
        name="norm_proj_t" if wt is not None else "norm_proj",
    )(*args)
    return res if wt is not None else res[0]


def _kvf_kernel(x_ref, g_ref, wk_ref, wvt_ref, wf_ref, bf_ref, sel_ref, k_ref, vt_ref, dec_ref, carry_ref):
    @pl.when(pl.program_id(1) == 0)
    def _():
        carry_ref[...] = jnp.zeros_like(carry_ref)

    xn = _rms(x_ref[...], g_ref[...], RMS_EPS).astype(jnp.bfloat16)
    k_ref[...] = jnp.dot(xn, wk_ref[...], preferred_element_type=jnp.float32).astype(k_ref.dtype)
    _store_vt(vt_ref, lax.dot_general(wvt_ref[...], xn, NT_DIMS, preferred_element_type=jnp.float32))
    fl = jnp.dot(xn, wf_ref[...], preferred_element_type=jnp.float32) + bf_ref[...]
    ls = jnp.minimum(fl, 0.0) - jnp.log1p(jnp.exp(-jnp.abs(fl)))
    tm = ls.shape[0]
    row = lax.broadcasted_iota(jnp.int32, ls.shape, 0)
    sh = 1
    while sh < tm:
        ls = ls + jnp.where(row >= sh, pltpu.roll(ls, sh, 0), 0.0)
        sh *= 2
    c = ls + carry_ref[...]
    carry_ref[...] = c[tm - 1:tm, :]
    x = c * (-LOG2E)
    hi = x.astype(jnp.bfloat16)
    r1 = x - hi.astype(jnp.float32)
    mid = r1.astype(jnp.bfloat16)
    lo = (r1 - mid.astype(jnp.float32)).astype(jnp.bfloat16)
    limbs = jnp.concatenate([hi, mid, lo], axis=1)
    dec_ref[...] = jnp.dot(limbs, sel_ref[...], preferred_element_type=jnp.float32).astype(dec_ref.dtype)


def _decay_selector():
    sel = np.zeros((DECAY_LIMBS * LANES, LANES), np.float32)
    for h in range(B_HEADS):
        for x in range(DECAY_LIMBS):
            sel[x * LANES + h, DECAY_LIMBS * h + x] = 1.0
    return jnp.asarray(sel, jnp.bfloat16)


def _shared_kv(h, gain, w_kvf, b_f):
    B, S, D = h.shape
    HD = B_HEADS * B_HEAD_DIM
    tm = ROW_TILE
    wk = w_kvf[:, :HD].astype(jnp.bfloat16)
    wvt = w_kvf[:, HD:2 * HD].T.astype(jnp.bfloat16)
    wf = jnp.pad(w_kvf[:, 2 * HD:], ((0, 0), (0, LANES - B_HEADS))).astype(jnp.bfloat16)
    bf = jnp.pad(b_f, (0, LANES - B_HEADS)).reshape(1, LANES)
    sel = _decay_selector()
    return pl.pallas_call(
        _kvf_kernel,
        grid=(B, S // tm),
        in_specs=[
            pl.BlockSpec((None, tm, D), lambda b, i: (b, i, 0)),
            _resident((1, D)),
            _resident((D, HD)),
            _resident((HD, D)),
            _resident((D, LANES)),
            _resident((1, LANES)),
            _resident(sel.shape),
        ],
        out_specs=[
            pl.BlockSpec((None, tm, HD), lambda b, i: (b, i, 0)),
            pl.BlockSpec((None, HD // LANES * VT_ROWS, tm), lambda b, i: (b, 0, i)),
            pl.BlockSpec((None, tm, LANES), lambda b, i: (b, i, 0)),
        ],
        out_shape=[
            jax.ShapeDtypeStruct((B, S, HD), jnp.bfloat16),
            jax.ShapeDtypeStruct((B, HD // LANES * VT_ROWS, S), jnp.bfloat16),
            jax.ShapeDtypeStruct((B, S, LANES), jnp.bfloat16),
        ],
        scratch_shapes=[pltpu.VMEM((1, LANES), jnp.float32)],
        compiler_params=_params(("parallel", "arbitrary")),
        name="shared_kvf",
    )(h, gain.reshape(1, D), wk, wvt, wf, bf, sel)


def _bias_kernel(rel_ref, d_ref, bkt_ref, out_ref):
    h = pl.program_id(0)
    d = d_ref[...]
    bkt = bkt_ref[...]
    acc = jnp.zeros(d.shape, jnp.float32)
    for j in range(N_BUCKETS):
        acc = jnp.where(bkt == j, rel_ref[j, h], acc)
    far = rel_ref[N_BUCKETS - 1, h]
    out_ref[...] = jnp.where(d >= 0, (acc - far) * LOG2E, NEG_INF)


def _t5_bucket(dist):
    max_exact = N_BUCKETS // 2
    d = jnp.maximum(dist, 0)
    log_ratio = jnp.log(jnp.maximum(d, 1).astype(jnp.float32) / max_exact) / math.log(MAX_DISTANCE / max_exact)
    large = jnp.minimum(max_exact + (log_ratio * (N_BUCKETS - max_exact)).astype(jnp.int32), N_BUCKETS - 1)
    return jnp.where(d < max_exact, d, large)


def _bias_tables(rel_bias):
    T = ATTN_T
    assert T >= MAX_DISTANCE
    key = jnp.arange(T)[:, None]
    qry = jnp.arange(T)[None, :]
    dist = jnp.stack([T + qry - key, qry - key]).astype(jnp.int32)
    bkt = _t5_bucket(dist).astype(jnp.int32)
    return pl.pallas_call(
        _bias_kernel,
        grid=(A_HEADS,),
        in_specs=[
            pl.BlockSpec(memory_space=pltpu.SMEM),
            _resident((2, T, T)),
            _resident((2, T, T)),
        ],
        out_specs=pl.BlockSpec((None, 2, T, T), lambda h: (h, 0, 0, 0)),
        out_shape=jax.ShapeDtypeStruct((A_HEADS, 2, T, T), jnp.float32),
        compiler_params=_params(("parallel",)),
        name="t5_bias_tiles",
    )(rel_bias, dist, bkt)


def _attn_kernel(*refs, diff, lam_init):
    if diff:
        q_ref, k_ref, vt_ref, bias_ref, lam_ref, g_ref, o_ref, *scr = refs
    else:
        q_ref, k_ref, vt_ref, dec_ref, ones_ref, o_ref, *scr = refs
    s_slots = scr[0:2]
    mx_slots = scr[2:4]
    streams = (scr[4:6], scr[6:8])
    T = ATTN_T
    half = LANES // 2
    qi = pl.program_id(2)

    q = q_ref[...]
    lane = lax.broadcasted_iota(jnp.int32, q.shape, 1)
    zero = jnp.zeros_like(q)
    qz = [jnp.where(lane < half, q, zero), jnp.where(lane >= half, q, zero)]
    if not diff:
        for j in range(2):
            qz[j] = jnp.concatenate([qz[j], ones_ref[j]], axis=1)
        key = lax.broadcasted_iota(jnp.int32, (T, T), 0)
        qry = lax.broadcasted_iota(jnp.int32, (T, T), 1)
        diag_mask = jnp.where(key <= qry, 0.0, NEG_INF)

    def logits(t, slot, tile_bias):
        kt = k_ref[t * T:(t + 1) * T, :]
        if not diff:
            kt = jnp.concatenate([kt, dec_ref[t * T:(t + 1) * T, :]], axis=1)
        for j in range(2):
            s = lax.dot_general(kt, qz[j], NT_DIMS, preferred_element_type=jnp.float32)
            if tile_bias is not None:
                s = s + tile_bias
            s_slots[slot][j] = s
            mx_slots[slot][j] = jnp.max(s, axis=0, keepdims=True)

    def softmax_pv(t, slot):
        vt = vt_ref[:, t * T:(t + 1) * T]
        for j, (m_ref, a_ref) in enumerate(streams):
            m_new = mx_slots[slot][j]
            if t > 0:
                m_old = m_ref[...]
                m_new = jnp.maximum(m_old, m_new)
            p = jnp.exp2((s_slots[slot][j] - m_new).astype(jnp.bfloat16))
            pv = jnp.dot(vt, p, preferred_element_type=jnp.float32)
            a_ref[...] = pv if t == 0 else jnp.exp2(m_old - m_new) * a_ref[...] + pv
            m_ref[...] = m_new

    def tile_bias(v, t):
        if diff:
            return bias_ref[1 - (v - t)] if v - t < 2 else None
        return diag_mask if t == v else None

    def walk(v):
        logits(0, 0, tile_bias(v, 0))
        for t in range(v + 1):
            if t < v:
                logits(t + 1, (t + 1) % 2, tile_bias(v, t + 1))
            softmax_pv(t, t % 2)

    for v in range(k_ref.shape[0] // T):
        pl.when(qi == v)(functools.partial(walk, v))

    (_, a0_ref), (_, a1_ref) = streams
    o0 = a0_ref[0:LANES, :] / a0_ref[LANES:LANES + 1, :]
    o1 = a1_ref[0:LANES, :] / a1_ref[LANES:LANES + 1, :]
    if diff:
        lam = (jnp.exp(jnp.sum(lam_ref[0:1, :] * lam_ref[1:2, :], keepdims=True))
               - jnp.exp(jnp.sum(lam_ref[2:3, :] * lam_ref[3:4, :], keepdims=True))
               + lam_init)
        o = o0 - lam * o1
        ms = jnp.mean(o * o, axis=0, keepdims=True)
        o = o * lax.rsqrt(ms + SUBLN_EPS) * g_ref[...] * (1.0 - lam_init)
    else:
        row = lax.broadcasted_iota(jnp.int32, o0.shape, 0)
        o = jnp.where(row < half, o0, o1)
    o_ref[...] = o.T.astype(o_ref.dtype)


def _attn_scratch():
    T = ATTN_T
    logit_slots = [pltpu.VMEM((2, T, T), jnp.float32)] * 2
    max_slots = [pltpu.VMEM((2, 1, T), jnp.float32)] * 2
    one = [pltpu.VMEM((1, T), jnp.float32), pltpu.VMEM((VT_ROWS, T), jnp.float32)]
    return logit_slots + max_slots + one + one


def _diff_attention(qk, vt, bias_tab, lam_vecs, subln_g, lam_init):
    B, S, _ = qk.shape
    T = ATTN_T
    H = A_HEADS
    return pl.pallas_call(
        functools.partial(_attn_kernel, diff=True, lam_init=lam_init),
        grid=(H, B, S // T),
        in_specs=[
            pl.BlockSpec((None, T, LANES), lambda h, b, i: (b, i, h)),
            pl.BlockSpec((None, S, LANES), lambda h, b, i: (b, 0, H + h)),
            pl.BlockSpec((None, VT_ROWS, S), lambda h, b, i: (b, h, 0)),
            pl.BlockSpec((None, 2, T, T), lambda h, b, i: (h, 0, 0, 0)),
            _resident((4, A_HEAD_DIM)),
            _resident((LANES, 1)),
        ],
        out_specs=pl.BlockSpec((None, T, LANES), lambda h, b, i: (b, i, h)),
        out_shape=jax.ShapeDtypeStruct((B, S, H * LANES), jnp.bfloat16),
        scratch_shapes=_attn_scratch(),
        compiler_params=_params(("parallel", "parallel", "arbitrary")),
        name="diff_attn",
    )(qk, qk, vt, bias_tab, lam_vecs, subln_g.reshape(LANES, 1))


def _fox_attention(q, k, vt, decay):
    B, S, _ = q.shape
    T = ATTN_T
    G = B_HEADS // 2
    ones = np.zeros((G, 2, T, LANES), np.float32)
    for g in range(G):
        for j in range(2):
            ones[g, j, :, DECAY_LIMBS * (2 * g + j):DECAY_LIMBS * (2 * g + j + 1)] = 1.0
    return pl.pallas_call(
        functools.partial(_attn_kernel, diff=False, lam_init=0.0),
        grid=(G, B, S // T),
        in_specs=[
            pl.BlockSpec((None, T, LANES), lambda g, b, i: (b, i, g)),
            pl.BlockSpec((None, S, LANES), lambda g, b, i: (b, 0, g)),
            pl.BlockSpec((None, VT_ROWS, S), lambda g, b, i: (b, g, 0)),
            pl.BlockSpec((None, S, LANES), lambda g, b, i: (b, 0, 0)),
            pl.BlockSpec((None, 2, T, LANES), lambda g, b, i: (g, 0, 0, 0)),
        ],
        out_specs=pl.BlockSpec((None, T, LANES), lambda g, b, i: (b, i, g)),
        out_shape=jax.ShapeDtypeStruct((B, S, G * LANES), jnp.bfloat16),
        scratch_shapes=_attn_scratch(),
        compiler_params=_params(("parallel", "parallel", "arbitrary")),
        name="fox_attn",
    )(q, k, vt, decay, jnp.asarray(ones, jnp.bfloat16))


def _oproj_kernel(o_ref, w_ref, h_ref, g_ref, out_ref):
    a = jnp.dot(o_ref[...], w_ref[...], preferred_element_type=jnp.float32)
    out_ref[...] = h_ref[...] + _rms(a, g_ref[...], RMS_EPS)


def _out_project(o, w, h, gain):
    B, S, D = h.shape
    tm = ROW_TILE
    row = pl.BlockSpec((None, tm, D), lambda b, i: (b, i, 0))
    return pl.pallas_call(
        _oproj_kernel,
        grid=(B, S // tm),
        in_specs=[row, _resident((D, D)), row, _resident((1, D))],
        out_specs=row,
        out_shape=jax.ShapeDtypeStruct((B, S, D), jnp.float32),
        compiler_params=_params(("parallel", "parallel")),
        name="out_proj",
    )(o, w, h, gain.reshape(1, D))


def _ffn_kernel(h_ref, halo_ref, gpre_ref, win_ref, cw_ref, cb_ref, wout_ref, gpost_ref,
                out_ref, xe_ref, acc_ref, u0_ref, u1_ref):
    F = D_FF
    FC = FFN_CHUNK
    n_chunks = F // FC
    u_slots = (u0_ref, u1_ref)
    h = h_ref[...]
    gpre = gpre_ref[...]
    first = (pl.program_id(1) == 0).astype(jnp.float32)
    xe_ref[0:HALO, :] = (_rms(halo_ref[...], gpre, RMS_EPS) * (1.0 - first)).astype(xe_ref.dtype)
    xe_ref[HALO:, :] = _rms(h, gpre, RMS_EPS).astype(xe_ref.dtype)
    acc_ref[...] = jnp.zeros_like(acc_ref)

    def conv(u_ref, w, b):
        tm = u_ref.shape[0] - HALO
        return (u_ref[pl.ds(HALO, tm), :] * w[2:3, :] + u_ref[pl.ds(HALO - 1, tm), :] * w[1:2, :]
                + u_ref[pl.ds(HALO - 2, tm), :] * w[0:1, :] + b)

    def offsets(c):
        return pl.multiple_of(c * FC, FC), pl.multiple_of(F + c * FC, LANES)

    def up_proj(c, slot):
        og, ov = offsets(c)
        xe = xe_ref[...]
        u_slots[slot][0] = jnp.dot(xe, win_ref[:, pl.ds(og, FC)], preferred_element_type=jnp.float32)
        u_slots[slot][1] = jnp.dot(xe, win_ref[:, pl.ds(ov, FC)], preferred_element_type=jnp.float32)

    def conv_act_down(c, slot):
        og, ov = offsets(c)
        gate = conv(u_slots[slot].at[0], cw_ref[:, pl.ds(og, FC)], cb_ref[:, pl.ds(og, FC)])
        val = conv(u_slots[slot].at[1], cw_ref[:, pl.ds(ov, FC)], cb_ref[:, pl.ds(ov, FC)])
        act = (jax.nn.gelu(gate) * val).astype(jnp.bfloat16)
        acc_ref[...] += jnp.dot(act, wout_ref[pl.ds(og, FC), :], preferred_element_type=jnp.float32)

    def pair(i, carry):
        up_proj(2 * i + 1, 1)
        conv_act_down(2 * i, 0)
        up_proj(2 * i + 2, 0)
        conv_act_down(2 * i + 1, 1)
        return carry

    assert n_chunks % 2 == 1
    up_proj(0, 0)
    lax.fori_loop(0, n_chunks // 2, pair, 0)
    conv_act_down(n_chunks - 1, 0)
    out_ref[...] = h + _rms(acc_ref[...], gpost_ref[...], RMS_EPS)


def _conv_ffn(h, gpre, w_in, conv_w, conv_b, w_out, gpost):
    B, S, D = h.shape
    F = D_FF
    tm = ROW_TILE
    assert F % FFN_CHUNK == 0 and tm % HALO == 0
    row = pl.BlockSpec((None, tm, D), lambda b, i: (b, i, 0))
    halo = pl.BlockSpec((None, HALO, D), lambda b, i: (b, jnp.maximum(i * (tm // HALO) - 1, 0), 0))
    return pl.pallas_call(
        _ffn_kernel,
        grid=(B, S // tm),
        in_specs=[row, halo, _resident((1, D)), _resident((D, 2 * F)), _resident((3, 2 * F)),
                  _resident((1, 2 * F)), _resident((F, D)), _resident((1, D))],
        out_specs=row,
        out_shape=jax.ShapeDtypeStruct((B, S, D), jnp.float32),
        scratch_shapes=[pltpu.VMEM((tm + HALO, D), jnp.bfloat16), pltpu.VMEM((tm, D), jnp.float32)]
        + [pltpu.VMEM((2, tm + HALO, FFN_CHUNK), jnp.float32)] * 2,
        compiler_params=_params(("parallel", "parallel")),
        name="conv_ffn",
    )(h, h, gpre.reshape(1, D), w_in.astype(jnp.bfloat16), conv_w, conv_b.reshape(1, 2 * F),
      w_out.astype(jnp.bfloat16), gpost.reshape(1, D))


def kernel(x, rel_bias, a_norm_pre, a_norm_post, a_w_qkv, a_lam_q1, a_lam_k1, a_lam_q2, a_lam_k2, a_subln, a_w_o, kv_norm, w_kvf, b_f, b_norm_pre, b_norm_post, b_w_q, b_w_o, ffn_norm_pre, ffn_norm_post, ffn_w_in, ffn_conv_w, ffn_conv_b, ffn_w_out):
    B, S, D = x.shape
    q_scale = LOG2E * A_HEAD_DIM ** -0.5
    nqk = 2 * D
    a_colscale = jnp.concatenate([jnp.full((D,), q_scale, jnp.float32), jnp.ones((D,), jnp.float32)])
    b_colscale = jnp.full((D,), LOG2E * B_HEAD_DIM ** -0.5, jnp.float32)
    bias_tab = _bias_tables(rel_bias)

    h = x
    for l in range(DEPTH):
        if l < N_A_LAYERS:
            w = a_w_qkv[l]
            qk, vt = _project(h, a_norm_pre[l], w[:, :nqk].astype(jnp.bfloat16), a_colscale,
                              wt=w[:, nqk:].T.astype(jnp.bfloat16))
            lam_vecs = jnp.stack([a_lam_q1[l], a_lam_k1[l], a_lam_q2[l], a_lam_k2[l]])
            o = _diff_attention(qk, vt, bias_tab, lam_vecs, a_subln[l], _lambda_init(l))
            h = _out_project(o, a_w_o[l].astype(jnp.bfloat16), h, a_norm_post[l])
        else:
            j = l - N_A_LAYERS
            if j == 0:
                k_sh, vt_sh, decay_sh = _shared_kv(h, kv_norm, w_kvf, b_f)
            q = _project(h, b_norm_pre[j], b_w_q[j].astype(jnp.bfloat16), b_colscale)
            o = _fox_attention(q, k_sh, vt_sh, decay_sh)
            h = _out_project(o, b_w_o[j].astype(jnp.bfloat16), h, b_norm_post[j])
        h = _conv_ffn(h, ffn_norm_pre[l], ffn_w_in[l], ffn_conv_w[l], ffn_conv_b[l], ffn_w_out[l],
                      ffn_norm_post[l])
    return h
```
